```python
import jax, jax.numpy as jnp
from jax import lax
import numpy as np

D_MODEL = 1024
BATCH = 8
SEQ = 4096
DEPTH = 2

N_MIXERS = 2
N_MEM = 256
MEM_HEADS = 4
MEM_HD = 64
SB_HEADS = 12
SB_HD = 64
SB_BLOCK = 128
HG_HEADS = 6
HG_DK = 128
HG_DV = 128
HG_CHUNK = 64
D_FF = 4 * D_MODEL
D_MIX = SB_HEADS * SB_HD + MEM_HEADS * MEM_HD
SB_PROJ = 3 * SB_HEADS * SB_HD + MEM_HEADS * MEM_HD
HG_PROJ = 2 * HG_HEADS * HG_DK + 2 * HG_HEADS * HG_DV + MEM_HEADS * MEM_HD
N_LAYERS_A = (DEPTH + N_MIXERS - 1) // N_MIXERS
N_LAYERS_B = (DEPTH + N_MIXERS - 2) // N_MIXERS
DN_ALPHA = (2 * DEPTH) ** 0.25
DN_BETA = (8 * DEPTH) ** -0.25
LN_EPS = 1e-5
RMS_EPS = 1e-6

kernel_name = "hybrid_stickbreak_hgrn2_memxattn_deepnorm"


def layer_norm(x, g, b):
    xf = x.astype(jnp.float32)
    mu = jnp.mean(xf, axis=-1, keepdims=True)
    var = jnp.mean(jnp.square(xf - mu), axis=-1, keepdims=True)
    return ((xf - mu) * lax.rsqrt(var + LN_EPS) * g + b).astype(x.dtype)


def split_heads(t, h):
    b, s, _ = t.shape
    return t.reshape(b, s, h, -1).transpose(0, 2, 1, 3)


def merge_heads(t):
    b, h, s, d = t.shape
    return t.transpose(0, 2, 1, 3).reshape(b, s, h * d)


def memory_attention(q_mem, mem, w_mem_kv):
    k, v = jnp.split(mem @ w_mem_kv, 2, axis=-1)
    q = split_heads(q_mem, MEM_HEADS)
    k = split_heads(k, MEM_HEADS)
    v = split_heads(v, MEM_HEADS)
    s = jnp.einsum('bhtd,bhmd->bhtm', q, k).astype(jnp.float32) * (MEM_HD ** -0.5)
    p = jax.nn.softmax(s, axis=-1).astype(v.dtype)
    return merge_heads(jnp.einsum('bhtm,bhmd->bhtd', p, v))


def stick_breaking_attention(q, k, v):
    seq = q.shape[2]
    scale = SB_HD ** -0.5
    outs = []
    for blk in range(seq // SB_BLOCK):
        t0 = blk * SB_BLOCK
        t1 = t0 + SB_BLOCK
        qb = q[:, :, t0:t1]
        kb = k[:, :, :t1]
        vb = v[:, :, :t1]
        z = jnp.einsum('bhtd,bhsd->bhts', qb, kb).astype(jnp.float32) * scale
        strict = jnp.arange(t1)[None, :] < jnp.arange(t0, t1)[:, None]
        log_stay = jnp.where(strict, jax.nn.log_sigmoid(-z), 0.0)
        later = lax.cumsum(log_stay, axis=3, reverse=True) - log_stay
        w = jnp.where(strict, jnp.exp(jax.nn.log_sigmoid(z) + later), 0.0)
        outs.append(jnp.einsum('bhts,bhsd->bhtd', w.astype(vb.dtype), vb))
    return jnp.concatenate(outs, axis=2)


def hgrn2_recurrence(q, k, v, log_f):
    b, h, s, dk = q.shape
    dv = v.shape[-1]
    nc = s // HG_CHUNK

    def chunks(t):
        return t.reshape(b, h, nc, HG_CHUNK, t.shape[-1]).transpose(2, 0, 1, 3, 4)

    causal = jnp.tril(jnp.ones((HG_CHUNK, HG_CHUNK), dtype=bool))[:, :, None]

    def step(state, inp):
        qc, kc, vc, gc = inp
        G = jnp.cumsum(gc.astype(jnp.float32), axis=2)
        G_last = G[:, :, -1:]
        o_inter = jnp.einsum('bhck,bhkv->bhcv', qc * jnp.exp(G), state)
        decay = jnp.exp(jnp.where(causal, G[:, :, :, None, :] - G[:, :, None, :, :], -jnp.inf))
        scores = jnp.einsum('bhtk,bhsk,bhtsk->bhts', qc, kc, decay)
        o_intra = jnp.einsum('bhts,bhsv->bhtv', scores, vc)
        k_dec = kc * jnp.exp(G_last - G)
        new_state = jnp.exp(G_last[:, :, 0])[..., None] * state + jnp.einsum('bhck,bhcv->bhkv', k_dec, vc)
        return new_state, o_inter + o_intra

    state0 = jnp.zeros((b, h, dk, dv), jnp.float32)
    _, o = lax.scan(step, state0, (chunks(q), chunks(k), chunks(v), chunks(log_f)))
    return o.transpose(1, 2, 0, 3, 4).reshape(b, h, s, dv)


def hgrn2_mixer(proj, lb, gnorm_g):
    wk = HG_HEADS * HG_DK
    wv = HG_HEADS * HG_DV
    q = proj[..., :wk]
    f_pre = proj[..., wk:2 * wk].astype(jnp.float32)
    i = proj[..., 2 * wk:2 * wk + wv]
    gate = proj[..., 2 * wk + wv:]
    log_f = jnp.logaddexp(jnp.log(lb), jnp.log1p(-lb) + jax.nn.log_sigmoid(f_pre))
    k = ((1.0 - lb) * jax.nn.sigmoid(-f_pre)).astype(q.dtype)
    o = hgrn2_recurrence(split_heads(q, HG_HEADS), split_heads(k, HG_HEADS),
                         split_heads(i, HG_HEADS), split_heads(log_f, HG_HEADS))
    of = o.astype(jnp.float32)
    of = of * lax.rsqrt(jnp.mean(jnp.square(of), axis=-1, keepdims=True) + RMS_EPS)
    o = merge_heads(of) * gnorm_g
    return (o * jax.nn.silu(gate.astype(jnp.float32))).astype(proj.dtype)


def hgrn2_lower_bounds(lower_bounds):
    p = jax.nn.softmax(lower_bounds.astype(jnp.float32), axis=0)
    return jnp.cumsum(p, axis=0) - p[0:1]


def setup_inputs(seed: int = 0) -> dict:
    key = jax.random.key(seed)
    ks = jax.random.split(key, 16)
    d_hg = HG_HEADS * HG_DK
    nrm = jax.random.normal
    return {
        "x": nrm(ks[0], (BATCH, SEQ, D_MODEL), jnp.float32),
        "mem": nrm(ks[1], (BATCH, N_MEM, D_MODEL), jnp.float32),
        "w_in_sb": nrm(ks[2], (N_LAYERS_A, D_MODEL, SB_PROJ), jnp.float32) * D_MODEL ** -0.5,
        "w_in_hg": nrm(ks[3], (N_LAYERS_B, D_MODEL, HG_PROJ), jnp.float32) * D_MODEL ** -0.5,
        "w_mem_kv": nrm(ks[4], (DEPTH, D_MODEL, 2 * MEM_HEADS * MEM_HD), jnp.float32) * D_MODEL ** -0.5,
        "lower_bounds": 1.0 + 0.1 * nrm(ks[5], (DEPTH, d_hg), jnp.float32),
        "hg_norm_g": 1.0 + 0.05 * nrm(ks[6], (N_LAYERS_B, HG_HEADS * HG_DV), jnp.float32),
        "w_out": nrm(ks[7], (DEPTH, D_MIX, D_MODEL), jnp.float32) * (D_MIX ** -0.5 * DN_BETA),
        "ln_mix_g": 1.0 + 0.05 * nrm(ks[8], (DEPTH, D_MODEL), jnp.float32),
        "ln_mix_b": 0.02 * nrm(ks[9], (DEPTH, D_MODEL), jnp.float32),
        "w_up": nrm(ks[10], (DEPTH, D_MODEL, D_FF), jnp.float32) * D_MODEL ** -0.5,
        "w_down": nrm(ks[11], (DEPTH, D_FF, D_MODEL), jnp.float32) * (D_FF ** -0.5 * DN_BETA),
        "ln_ffn_g": 1.0 + 0.05 * nrm(ks[12], (DEPTH, D_MODEL), jnp.float32),
        "ln_ffn_b": 0.02 * nrm(ks[13], (DEPTH, D_MODEL), jnp.float32),
    }


def reference(x, mem, w_in_sb, w_in_hg, w_mem_kv, lower_bounds, hg_norm_g, w_out,
              ln_mix_g, ln_mix_b, w_up, w_down, ln_ffn_g, ln_ffn_b):
    lbs = hgrn2_lower_bounds(lower_bounds)
    w_sb = SB_HEADS * SB_HD
    for layer in range(DEPTH):
        slot = layer // N_MIXERS
        if layer % N_MIXERS == 0:
            proj = x @ w_in_sb[slot]
            q = split_heads(proj[..., :w_sb], SB_HEADS)
            k = split_heads(proj[..., w_sb:2 * w_sb], SB_HEADS)
            v = split_heads(proj[..., 2 * w_sb:3 * w_sb], SB_HEADS)
            q_mem = proj[..., 3 * w_sb:]
            mix = merge_heads(stick_breaking_attention(q, k, v))
        else:
            proj = x @ w_in_hg[slot]
            split = 2 * HG_HEADS * HG_DK + 2 * HG_HEADS * HG_DV
            q_mem = proj[..., split:]
            mix = hgrn2_mixer(proj[..., :split], lbs[layer], hg_norm_g[slot])
        mem_out = memory_attention(q_mem, mem, w_mem_kv[layer])
        y = jnp.concatenate([mix, mem_out.astype(mix.dtype)], axis=-1) @ w_out[layer]
        x = layer_norm(DN_ALPHA * x + y, ln_mix_g[layer], ln_mix_b[layer])
        h = jnp.square(jax.nn.relu(x @ w_up[layer]))
        x = layer_norm(DN_ALPHA * x + h @ w_down[layer], ln_ffn_g[layer], ln_ffn_b[layer])
    return x
```

```python
import functools

import jax
import jax.numpy as jnp
from jax import lax
from jax.experimental import pallas as pl
from jax.experimental.pallas import tpu as pltpu

D_MODEL = 1024
DEPTH = 2
N_MIXERS = 2
N_MEM = 256
MEM_HEADS = 4
MEM_HD = 64
SB_HEADS = 12
SB_HD = 64
HG_HEADS = 6
HG_DK = 128
HG_DV = 128
D_FF = 4 * D_MODEL
W_SB = SB_HEADS * SB_HD
W_HG = HG_HEADS * HG_DK
W_MEM = MEM_HEADS * MEM_HD
DN_ALPHA = (2 * DEPTH) ** 0.25
LN_EPS = 1e-5
RMS_EPS = 1e-6

F32 = jnp.float32
BF16 = jnp.bfloat16

VMEM_LIMIT_BYTES = 48 * 1024 * 1024

SB_TQ = 256
SB_TK = 256
SB_ZERO_LOG = -104.0
HG_CHUNK = 64
HG_SAFE_RANGE = 80.0
FF_CHUNK = 512

_NT = (((1,), (1,)), ((), ()))
_TN = (((0,), (0,)), ((), ()))


def _softplus(z):
    return jnp.maximum(z, 0.0) + jnp.log(1.0 + jnp.exp(-jnp.abs(z)))


def _split_bf16(a, parts):
    out = []
    rem = a
    for _ in range(parts - 1):
        p = rem.astype(BF16)
        out.append(p)
        rem = rem - p.astype(F32)
    out.append(rem.astype(BF16))
    return out


def _layer_norm(r, g, b):
    mu = jnp.mean(r, axis=-1, keepdims=True)
    c = r - mu
    var = jnp.mean(c * c, axis=-1, keepdims=True)
    return c * lax.rsqrt(var + LN_EPS) * g + b


def _proj_kernel(x_ref, w_ref, *out_refs, widths, scales):
    xb = x_ref[...].astype(BF16)
    off = 0
    for o_ref, wd, sc in zip(out_refs, widths, scales):
        y = jnp.dot(xb, w_ref[:, off:off + wd], preferred_element_type=F32)
        if sc != 1.0:
            y = y * sc
        o_ref[...] = y.astype(o_ref.dtype)
        off += wd


def _proj(x2d, w, widths, scales, tm):
    t, k = x2d.shape
    n = w.shape[1]
    assert sum(widths) == n and t % tm == 0
    return pl.pallas_call(
        functools.partial(_proj_kernel, widths=widths, scales=scales),
        grid=(t // tm,),
        in_specs=[pl.BlockSpec((tm, k), lambda i: (i, 0)),
                  pl.BlockSpec((k, n), lambda i: (0, 0))],
        out_specs=[pl.BlockSpec((tm, wd), lambda i: (i, 0)) for wd in widths],
        out_shape=[jax.ShapeDtypeStruct((t, wd), BF16) for wd in widths],
        compiler_params=pltpu.CompilerParams(
            dimension_semantics=("parallel",), vmem_limit_bytes=VMEM_LIMIT_BYTES),
        name="proj",
    )(x2d, w)


def _sb_tile(qh, kt, vt, tri, carry, acc, mask):
    z = lax.dot_general(qh, kt, _NT, preferred_element_type=F32)
    log_stay = -_softplus(z)
    log_beta = z + log_stay
    if mask is not None:
        log_stay = jnp.where(mask, log_stay, 0.0)
    hi, lo = _split_bf16(log_stay, 2)
    later = (jnp.dot(hi, tri, preferred_element_type=F32)
             + jnp.dot(lo, tri, preferred_element_type=F32))
    w = jnp.exp(log_beta + later + carry)
    if mask is not None:
        w = jnp.where(mask, w, 0.0)
    acc = acc + jnp.dot(w.astype(BF16), vt, preferred_element_type=F32)
    carry = carry + jnp.sum(log_stay, axis=-1, keepdims=True)
    return carry, acc


def _sb_kernel(q_ref, k_ref, v_ref, o_ref):
    qi = pl.program_id(2)
    q2 = q_ref[0]
    lane = lax.broadcasted_iota(jnp.int32, (1, 2 * SB_HD), 1)
    row = lax.broadcasted_iota(jnp.int32, (SB_TQ, SB_TK), 0)
    col = lax.broadcasted_iota(jnp.int32, (SB_TQ, SB_TK), 1)
    strict = col < row
    tri = (row > col).astype(BF16)

    q_start = pl.multiple_of(qi * SB_TQ, SB_TQ)
    k_diag = k_ref[0, pl.ds(q_start, SB_TK), :]
    v_diag = v_ref[0, pl.ds(q_start, SB_TK), :]

    accs = []
    for h in range(2):
        in_head = (lane >= h * SB_HD) & (lane < (h + 1) * SB_HD)
        qh = jnp.where(in_head, q2, jnp.zeros_like(q2))
        carry = jnp.zeros((SB_TQ, 1), F32)
        acc = jnp.zeros((SB_TQ, 2 * SB_HD), F32)
        carry, acc = _sb_tile(qh, k_diag, v_diag, tri, carry, acc, strict)

        def cond(c):
            j, _, _, go = c
            return jnp.logical_and(j >= 0, go)

        def body(c, qh=qh):
            j, carry, acc, _ = c
            start = pl.multiple_of(j * SB_TK, SB_TK)
            kt = k_ref[0, pl.ds(start, SB_TK), :]
            vt = v_ref[0, pl.ds(start, SB_TK), :]
            carry, acc = _sb_tile(qh, kt, vt, tri, carry, acc, None)
            return j - 1, carry, acc, jnp.max(carry) > SB_ZERO_LOG

        _, _, acc, _ = lax.while_loop(
            cond, body, (qi - 1, carry, acc, jnp.max(carry) > SB_ZERO_LOG))
        accs.append(acc)

    first = lane < SB_HD
    o_ref[0] = jnp.where(first, accs[0], accs[1]).astype(o_ref.dtype)


def _sb_attention(q, k, v):
    b, s, w = q.shape
    assert s % SB_TQ == 0 and SB_TQ == SB_TK
    pair = 2 * SB_HD
    return pl.pallas_call(
        _sb_kernel,
        grid=(b, w // pair, s // SB_TQ),
        in_specs=[pl.BlockSpec((1, SB_TQ, pair), lambda bi, hp, qi: (bi, qi, hp)),
                  pl.BlockSpec((1, s, pair), lambda bi, hp, qi: (bi, 0, hp)),
                  pl.BlockSpec((1, s, pair), lambda bi, hp, qi: (bi, 0, hp))],
        out_specs=pl.BlockSpec((1, SB_TQ, pair), lambda bi, hp, qi: (bi, qi, hp)),
        out_shape=jax.ShapeDtypeStruct((b, s, w), BF16),
        compiler_params=pltpu.CompilerParams(
            dimension_semantics=("parallel", "parallel", "arbitrary"),
            vmem_limit_bytes=VMEM_LIMIT_BYTES),
        name="sb_attention",
    )(q, k, v)


def _hgrn_kernel(q_ref, f_ref, i_ref, gate_ref, lbp_ref, gn_ref, o_ref,
                 st_ref, oi_ref, g_scr, k_scr, v_scr, *, layer):
    c = HG_CHUNK

    @pl.when(pl.program_id(1) == 0)
    def _():
        st_ref[...] = jnp.zeros_like(st_ref)

    rows = [lbp_ref[l:l + 1, :] for l in range(DEPTH)]
    mx = functools.reduce(jnp.maximum, rows)
    es = [jnp.exp(r - mx) for r in rows]
    den = functools.reduce(lambda a, b: a + b, es)
    ps = [e / den for e in es]
    cs = ps[0]
    for l in range(1, layer + 1):
        cs = cs + ps[l]
    lb_all = cs - ps[0]

    trow = lax.broadcasted_iota(jnp.int32, (c, c), 0)
    tcol = lax.broadcasted_iota(jnp.int32, (c, c), 1)
    tril = tcol <= trow
    tri_incl = tril.astype(BF16)
    row_id = lax.broadcasted_iota(jnp.int32, (c, HG_DK), 0)

    for h in range(HG_HEADS):
        sl = slice(h * HG_DK, (h + 1) * HG_DK)
        lb = lb_all[:, sl]
        q = q_ref[0, :, sl].astype(F32)
        z = f_ref[0, :, sl].astype(F32)
        v = i_ref[0, :, sl]
        a = jnp.log(lb)
        bb = jnp.log(1.0 - lb) + (-_softplus(-z))
        log_f = jnp.maximum(a, bb) + jnp.log(1.0 + jnp.exp(-jnp.abs(a - bb)))
        kk = (1.0 - lb) * jnp.exp(-_softplus(z))

        g_cum = sum(jnp.dot(tri_incl, p, preferred_element_type=F32)
                    for p in _split_bf16(log_f, 3))
        g_last = g_cum[c - 1:c, :]
        qg = (q * jnp.exp(g_cum)).astype(BF16)
        fast = jnp.min(g_last) >= -HG_SAFE_RANGE

        @pl.when(fast)
        def _():
            ke = (kk * jnp.exp(-g_cum)).astype(BF16)
            sc = lax.dot_general(qg, ke, _NT, preferred_element_type=F32)
            sc = jnp.where(tril, sc, 0.0)
            oi_ref[...] = jnp.dot(sc.astype(BF16), v, preferred_element_type=F32)

        @pl.when(jnp.logical_not(fast))
        def _():
            g_scr[...] = g_cum
            k_scr[...] = kk
            v_scr[...] = v.astype(F32)

            def step(s, acc):
                gs = g_scr[pl.ds(s, 1), :]
                dec = jnp.where(row_id >= s, jnp.exp(jnp.minimum(g_cum - gs, 0.0)), 0.0)
                sc = jnp.sum(q * k_scr[pl.ds(s, 1), :] * dec, axis=-1, keepdims=True)
                return acc + sc * v_scr[pl.ds(s, 1), :]

            oi_ref[...] = lax.fori_loop(0, c, step, jnp.zeros((c, HG_DV), F32))

        st = st_ref[h]
        o = oi_ref[...] + lax.dot_general(qg, st.astype(BF16), _NT, preferred_element_type=F32)
        kd = (kk * jnp.exp(g_last - g_cum)).astype(BF16)
        st_ref[h] = st * jnp.exp(g_last) + lax.dot_general(v, kd, _TN, preferred_element_type=F32)

        o = o * lax.rsqrt(jnp.mean(o * o, axis=-1, keepdims=True) + RMS_EPS)
        o = o * gn_ref[:, sl]
        gt = gate_ref[0, :, sl].astype(F32)
        o_ref[0, :, sl] = (o * (gt / (1.0 + jnp.exp(-gt)))).astype(o_ref.dtype)


def _hgrn_mixer(q, f, i, gate, lower_bounds, gnorm, layer):
    b, s, w = q.shape
    c = HG_CHUNK
    assert s % c == 0
    tok = pl.BlockSpec((1, c, w), lambda bi, ci: (bi, ci, 0))
    return pl.pallas_call(
        functools.partial(_hgrn_kernel, layer=layer),
        grid=(b, s // c),
        in_specs=[tok, tok, tok, tok,
                  pl.BlockSpec((DEPTH, w), lambda bi, ci: (0, 0)),
                  pl.BlockSpec((1, w), lambda bi, ci: (0, 0))],
        out_specs=tok,
        out_shape=jax.ShapeDtypeStruct((b, s, w), BF16),
        scratch_shapes=[pltpu.VMEM((HG_HEADS, HG_DV, HG_DK), F32),
                        pltpu.VMEM((c, HG_DV), F32),
                        pltpu.VMEM((c, HG_DK), F32),
                        pltpu.VMEM((c, HG_DK), F32),
                        pltpu.VMEM((c, HG_DV), F32)],
        compiler_params=pltpu.CompilerParams(
            dimension_semantics=("parallel", "arbitrary"),
            vmem_limit_bytes=VMEM_LIMIT_BYTES),
        name="hgrn2",
    )(q, f, i, gate, lower_bounds, gnorm)


def _post_kernel(x_ref, mix_ref, qm_ref, kv_ref, wo_ref, g_ref, b_ref, o_ref):
    qm = qm_ref[...]
    kmem = kv_ref[0, :, :W_MEM]
    vmem = kv_ref[0, :, W_MEM:]
    lane = lax.broadcasted_iota(jnp.int32, (1, W_MEM), 1)
    mem_out = jnp.zeros(qm.shape, F32)
    for h in range(MEM_HEADS):
        in_head = (lane >= h * MEM_HD) & (lane < (h + 1) * MEM_HD)
        qh = jnp.where(in_head, qm, jnp.zeros_like(qm))
        s = lax.dot_general(qh, kmem, _NT, preferred_element_type=F32)
        p = jnp.exp(s - jnp.max(s, axis=-1, keepdims=True))
        p = p / jnp.sum(p, axis=-1, keepdims=True)
        oh = jnp.dot(p.astype(BF16), vmem, preferred_element_type=F32)
        mem_out = jnp.where(in_head, oh, mem_out)
    y = (jnp.dot(mix_ref[...], wo_ref[:W_SB, :], preferred_element_type=F32)
         + jnp.dot(mem_out.astype(BF16), wo_ref[W_SB:, :], preferred_element_type=F32))
    o_ref[...] = _layer_norm(DN_ALPHA * x_ref[...] + y, g_ref[...], b_ref[...])


def _post(x2d, mix2d, qm2d, kv, w_out, g, b, seq, tm):
    t, d = x2d.shape
    assert seq % tm == 0
    per_b = seq // tm
    return pl.pallas_call(
        _post_kernel,
        grid=(t // tm,),
        in_specs=[pl.BlockSpec((tm, d), lambda i: (i, 0)),
                  pl.BlockSpec((tm, W_SB), lambda i: (i, 0)),
                  pl.BlockSpec((tm, W_MEM), lambda i: (i, 0)),
                  pl.BlockSpec((1, N_MEM, 2 * W_MEM), lambda i: (i // per_b, 0, 0)),
                  pl.BlockSpec((d, d), lambda i: (0, 0)),
                  pl.BlockSpec((1, d), lambda i: (0, 0)),
                  pl.BlockSpec((1, d), lambda i: (0, 0))],
        out_specs=pl.BlockSpec((tm, d), lambda i: (i, 0)),
        out_shape=jax.ShapeDtypeStruct((t, d), F32),
        compiler_params=pltpu.CompilerParams(
            dimension_semantics=("parallel",), vmem_limit_bytes=VMEM_LIMIT_BYTES),
        name="post_mixer",
    )(x2d, mix2d, qm2d, kv, w_out, g, b)


def _mlp_kernel(x_ref, wu_ref, wd_ref, g_ref, b_ref, o_ref):
    x = x_ref[...]
    xb = x.astype(BF16)
    y = jnp.zeros(x.shape, F32)
    for c0 in range(0, D_FF, FF_CHUNK):
        h = jnp.dot(xb, wu_ref[:, c0:c0 + FF_CHUNK], preferred_element_type=F32)
        h = jnp.maximum(h, 0.0)
        y = y + jnp.dot((h * h).astype(BF16), wd_ref[c0:c0 + FF_CHUNK, :],
                        preferred_element_type=F32)
    o_ref[...] = _layer_norm(DN_ALPHA * x + y, g_ref[...], b_ref[...])


def _mlp(x2d, w_up, w_down, g, b, tm):
    t, d = x2d.shape
    resident = dict(pipeline_mode=pl.Buffered(1))
    return pl.pallas_call(
        _mlp_kernel,
        grid=(t // tm,),
        in_specs=[pl.BlockSpec((tm, d), lambda i: (i, 0)),
                  pl.BlockSpec((d, D_FF), lambda i: (0, 0), **resident),
                  pl.BlockSpec((D_FF, d), lambda i: (0, 0), **resident),
                  pl.BlockSpec((1, d), lambda i: (0, 0)),
                  pl.BlockSpec((1, d), lambda i: (0, 0))],
        out_specs=pl.BlockSpec((tm, d), lambda i: (i, 0)),
        out_shape=jax.ShapeDtypeStruct((t, d), F32),
        compiler_params=pltpu.CompilerParams(
            dimension_semantics=("parallel",), vmem_limit_bytes=VMEM_LIMIT_BYTES),
        name="mlp",
    )(x2d, w_up, w_down, g, b)


def kernel(x, mem, w_in_sb, w_in_hg, w_mem_kv, lower_bounds, hg_norm_g, w_out,
           ln_mix_g, ln_mix_b, w_up, w_down, ln_ffn_g, ln_ffn_b):
    b, s, d = x.shape
    t = b * s
    tm = 512
    xf = x.reshape(t, d)
    mem2d = mem.reshape(b * mem.shape[1], d)
    qk_scale = SB_HD ** -0.5
    mem_scale = MEM_HD ** -0.5
    for layer in range(DEPTH):
        slot = layer // N_MIXERS
        if layer % N_MIXERS == 0:
            q, k, v, qm = _proj(xf, w_in_sb[slot].astype(BF16),
                                (W_SB, W_SB, W_SB, W_MEM), (qk_scale, 1.0, 1.0, mem_scale), tm)
            mix = _sb_attention(q.reshape(b, s, W_SB), k.reshape(b, s, W_SB),
                                v.reshape(b, s, W_SB))
        else:
            q, f, i, gate, qm = _proj(xf, w_in_hg[slot].astype(BF16),
                                      (W_HG, W_HG, W_HG, W_HG, W_MEM),
                                      (1.0, 1.0, 1.0, 1.0, mem_scale), tm)
            mix = _hgrn_mixer(q.reshape(b, s, W_HG), f.reshape(b, s, W_HG),
                              i.reshape(b, s, W_HG), gate.reshape(b, s, W_HG),
                              lower_bounds, hg_norm_g[slot].reshape(1, W_HG), layer)
        (kv,) = _proj(mem2d, w_mem_kv[layer].astype(BF16), (2 * W_MEM,), (1.0,), N_MEM)
        x1 = _post(xf, mix.reshape(t, W_SB), qm, kv.reshape(b, N_MEM, 2 * W_MEM),
                   w_out[layer].astype(BF16), ln_mix_g[layer].reshape(1, d),
                   ln_mix_b[layer].reshape(1, d), s, tm)
        xf = _mlp(x1, w_up[layer].astype(BF16), w_down[layer].astype(BF16),
                  ln_ffn_g[layer].reshape(1, d), ln_ffn_b[layer].reshape(1, d), tm)
    return xf.reshape(b, s, d)
```

```python
import functools

import jax
import jax.numpy as jnp
from jax import lax
from jax.experimental import pallas as pl
from jax.experimental.pallas import tpu as pltpu

D_MODEL = 1024
DEPTH = 2
N_MIXERS = 2
N_MEM = 256
MEM_HEADS = 4
MEM_HD = 64
SB_HEADS = 12
SB_HD = 64
HG_HEADS = 6
HG_DK = 128
HG_DV = 128
D_FF = 4 * D_MODEL
W_SB = SB_HEADS * SB_HD
W_HG = HG_HEADS * HG_DK
W_MEM = MEM_HEADS * MEM_HD
DN_ALPHA = (2 * DEPTH) ** 0.25
LN_EPS = 1e-5
RMS_EPS = 1e-6

F32 = jnp.float32
BF16 = jnp.bfloat16

VMEM_LIMIT_BYTES = 48 * 1024 * 1024

SB_TQ = 256
SB_TK = 256
SB_ZERO_LOG = -104.0
HG_CHUNK = 128
HG_SAFE_RANGE = 80.0
FF_CHUNK = 512

_NT = (((1,), (1,)), ((), ()))
_TN = (((0,), (0,)), ((), ()))


def _split_bf16(a, parts):
    out = []
    rem = a
    for _ in range(parts - 1):
        p = rem.astype(BF16)
        out.append(p)
        rem = rem - p.astype(F32)
    out.append(rem.astype(BF16))
    return out


def _layer_norm(r, g, b):
    mu = jnp.mean(r, axis=-1, keepdims=True)
    c = r - mu
    var = jnp.mean(c * c, axis=-1, keepdims=True)
    return c * lax.rsqrt(var + LN_EPS) * g + b


def _proj_kernel(x_ref, w_ref, *out_refs, widths, scales):
    xb = x_ref[...].astype(BF16)
    off = 0
    for o_ref, wd, sc in zip(out_refs, widths, scales):
        y = jnp.dot(xb, w_ref[:, off:off + wd], preferred_element_type=F32)
        if sc != 1.0:
            y = y * sc
        o_ref[...] = y.astype(o_ref.dtype)
        off += wd


def _proj(x2d, w, widths, scales, tm):
    t, k = x2d.shape
    n = w.shape[1]
    assert sum(widths) == n and t % tm == 0
    return pl.pallas_call(
        functools.partial(_proj_kernel, widths=widths, scales=scales),
        grid=(t // tm,),
        in_specs=[pl.BlockSpec((tm, k), lambda i: (i, 0)),
                  pl.BlockSpec((k, n), lambda i: (0, 0))],
        out_specs=[pl.BlockSpec((tm, wd), lambda i: (i, 0)) for wd in widths],
        out_shape=[jax.ShapeDtypeStruct((t, wd), BF16) for wd in widths],
        compiler_params=pltpu.CompilerParams(
            dimension_semantics=("parallel",), vmem_limit_bytes=VMEM_LIMIT_BYTES),
        name="proj",
    )(x2d, w)


def _sb_tile(qh, kt, vt, tri, carry, acc, mask):
    z = lax.dot_general(qh, kt, _NT, preferred_element_type=F32)
    log_stay = -(jnp.maximum(z, 0.0) + jnp.log(1.0 + jnp.exp(-jnp.abs(z))))
    log_beta = z + log_stay
    if mask is not None:
        log_stay = jnp.where(mask, log_stay, 0.0)
    hi, lo = _split_bf16(log_stay, 2)
    later = (jnp.dot(hi, tri, preferred_element_type=F32)
             + jnp.dot(lo, tri, preferred_element_type=F32))
    w = jnp.exp(log_beta + later + carry)
    if mask is not None:
        w = jnp.where(mask, w, 0.0)
    acc = acc + jnp.dot(w.astype(BF16), vt, preferred_element_type=F32)
    carry = carry + jnp.sum(log_stay, axis=-1, keepdims=True)
    return carry, acc


def _sb_kernel(q_ref, k_ref, v_ref, o_ref):
    qi = pl.program_id(2)
    q2 = q_ref[0]
    lane = lax.broadcasted_iota(jnp.int32, (1, 2 * SB_HD), 1)
    first = lane < SB_HD
    zero = jnp.zeros_like(q2)
    qs = jnp.concatenate([jnp.where(first, q2, zero), jnp.where(first, zero, q2)], axis=0)

    row = lax.broadcasted_iota(jnp.int32, (SB_TQ, SB_TK), 0)
    col = lax.broadcasted_iota(jnp.int32, (SB_TQ, SB_TK), 1)
    tri = (row > col).astype(BF16)
    strict = jnp.concatenate([col < row, col < row], axis=0)

    q_start = pl.multiple_of(qi * SB_TQ, SB_TQ)
    carry = jnp.zeros((2 * SB_TQ, 1), F32)
    acc = jnp.zeros((2 * SB_TQ, 2 * SB_HD), F32)
    carry, acc = _sb_tile(qs, k_ref[0, pl.ds(q_start, SB_TK), :],
                          v_ref[0, pl.ds(q_start, SB_TK), :], tri, carry, acc, strict)

    def cond(c):
        j, _, _, go = c
        return jnp.logical_and(j >= 0, go)

    def body(c):
        j, carry, acc, _ = c
        start = pl.multiple_of(j * SB_TK, SB_TK)
        carry, acc = _sb_tile(qs, k_ref[0, pl.ds(start, SB_TK), :],
                              v_ref[0, pl.ds(start, SB_TK), :], tri, carry, acc, None)
        return j - 1, carry, acc, jnp.max(carry) > SB_ZERO_LOG

    _, _, acc, _ = lax.while_loop(
        cond, body, (qi - 1, carry, acc, jnp.max(carry) > SB_ZERO_LOG))
    o_ref[0] = jnp.where(first, acc[:SB_TQ], acc[SB_TQ:]).astype(o_ref.dtype)


def _sb_attention(q, k, v):
    b, s, w = q.shape
    assert s % SB_TQ == 0 and SB_TQ == SB_TK
    pair = 2 * SB_HD
    return pl.pallas_call(
        _sb_kernel,
        grid=(b, w // pair, s // SB_TQ),
        in_specs=[pl.BlockSpec((1, SB_TQ, pair), lambda bi, hp, qi: (bi, qi, hp)),
                  pl.BlockSpec((1, s, pair), lambda bi, hp, qi: (bi, 0, hp)),
                  pl.BlockSpec((1, s, pair), lambda bi, hp, qi: (bi, 0, hp))],
        out_specs=pl.BlockSpec((1, SB_TQ, pair), lambda bi, hp, qi: (bi, qi, hp)),
        out_shape=jax.ShapeDtypeStruct((b, s, w), BF16),
        compiler_params=pltpu.CompilerParams(
            dimension_semantics=("parallel", "parallel", "arbitrary"),
            vmem_limit_bytes=VMEM_LIMIT_BYTES),
        name="sb_attention",
    )(q, k, v)


def _hgrn_kernel(q_ref, f_ref, i_ref, gate_ref, lbp_ref, gn_ref, o_ref,
                 st_ref, g_scr, k_scr, v_scr, *, layer):
    c = HG_CHUNK

    @pl.when(pl.program_id(1) == 0)
    def _():
        st_ref[...] = jnp.zeros_like(st_ref)

    rows = [lbp_ref[l:l + 1, :] for l in range(DEPTH)]
    mx = functools.reduce(jnp.maximum, rows)
    es = [jnp.exp(r - mx) for r in rows]
    den = functools.reduce(lambda a, b: a + b, es)
    ps = [e / den for e in es]
    cs = ps[0]
    for l in range(1, layer + 1):
        cs = cs + ps[l]
    lb_all = cs - ps[0]

    trow = lax.broadcasted_iota(jnp.int32, (c, c), 0)
    tcol = lax.broadcasted_iota(jnp.int32, (c, c), 1)
    tril = tcol <= trow
    tri_incl = tril.astype(BF16)

    g_min = None
    for h in range(HG_HEADS):
        sl = slice(h * HG_DK, (h + 1) * HG_DK)
        lb = lb_all[:, sl]
        z = f_ref[0, :, sl].astype(F32)
        t = jnp.log(1.0 + jnp.exp(-jnp.abs(z)))
        sp_pos = jnp.maximum(z, 0.0) + t
        sp_neg = jnp.maximum(-z, 0.0) + t
        a = jnp.log(lb)
        bb = jnp.log(1.0 - lb) - sp_neg
        log_f = jnp.maximum(a, bb) + jnp.log(1.0 + jnp.exp(-jnp.abs(a - bb)))
        k_scr[h] = (1.0 - lb) * jnp.exp(-sp_pos)
        hi, lo = _split_bf16(log_f, 2)
        g_cum = (jnp.dot(tri_incl, hi, preferred_element_type=F32)
                 + jnp.dot(tri_incl, lo, preferred_element_type=F32))
        g_scr[h] = g_cum
        g_last = g_cum[c - 1:c, :]
        g_min = g_last if g_min is None else jnp.minimum(g_min, g_last)
    fast = jnp.min(g_min) >= -HG_SAFE_RANGE

    def head(h, factorised):
        sl = slice(h * HG_DK, (h + 1) * HG_DK)
        q = q_ref[0, :, sl].astype(F32)
        v = i_ref[0, :, sl]
        g_cum = g_scr[h]
        kk = k_scr[h]
        g_last = g_cum[c - 1:c, :]
        qg = (q * jnp.exp(g_cum)).astype(BF16)
        if factorised:
            ke = (kk * jnp.exp(-g_cum)).astype(BF16)
            sc = lax.dot_general(qg, ke, _NT, preferred_element_type=F32)
            sc = jnp.where(tril, sc, 0.0)
            o = jnp.dot(sc.astype(BF16), v, preferred_element_type=F32)
        else:
            v_scr[...] = v.astype(F32)
            row_id = lax.broadcasted_iota(jnp.int32, (c, HG_DK), 0)

            def step(s, acc):
                gs = g_scr[h, pl.ds(s, 1), :]
                dec = jnp.where(row_id >= s, jnp.exp(jnp.minimum(g_cum - gs, 0.0)), 0.0)
                sc = jnp.sum(q * k_scr[h, pl.ds(s, 1), :] * dec, axis=-1, keepdims=True)
                return acc + sc * v_scr[pl.ds(s, 1), :]

            o = lax.fori_loop(0, c, step, jnp.zeros((c, HG_DV), F32))

        st = st_ref[h]
        o = o + lax.dot_general(qg, st.astype(BF16), _NT, preferred_element_type=F32)
        kd = (kk * jnp.exp(g_last - g_cum)).astype(BF16)
        st_ref[h] = st * jnp.exp(g_last) + lax.dot_general(v, kd, _TN, preferred_element_type=F32)

        o = o * lax.rsqrt(jnp.mean(o * o, axis=-1, keepdims=True) + RMS_EPS)
        o = o * gn_ref[:, sl]
        gt = gate_ref[0, :, sl].astype(F32)
        o_ref[0, :, sl] = (o * (gt / (1.0 + jnp.exp(-gt)))).astype(o_ref.dtype)

    @pl.when(fast)
    def _():
        for h in range(HG_HEADS):
            head(h, True)

    @pl.when(jnp.logical_not(fast))
    def _():
        for h in range(HG_HEADS):
            head(h, False)


def _hgrn_mixer(q, f, i, gate, lower_bounds, gnorm, layer):
    b, s, w = q.shape
    c = HG_CHUNK
    assert s % c == 0
    tok = pl.BlockSpec((1, c, w), lambda bi, ci: (bi, ci, 0))
    return pl.pallas_call(
        functools.partial(_hgrn_kernel, layer=layer),
        grid=(b, s // c),
        in_specs=[tok, tok, tok, tok,
                  pl.BlockSpec((DEPTH, w), lambda bi, ci: (0, 0)),
                  pl.BlockSpec((1, w), lambda bi, ci: (0, 0))],
        out_specs=tok,
        out_shape=jax.ShapeDtypeStruct((b, s, w), BF16),
        scratch_shapes=[pltpu.VMEM((HG_HEADS, HG_DV, HG_DK), F32),
                        pltpu.VMEM((HG_HEADS, c, HG_DK), F32),
                        pltpu.VMEM((HG_HEADS, c, HG_DK), F32),
                        pltpu.VMEM((c, HG_DV), F32)],
        compiler_params=pltpu.CompilerParams(
            dimension_semantics=("parallel", "arbitrary"),
            vmem_limit_bytes=VMEM_LIMIT_BYTES),
        name="hgrn2",
    )(q, f, i, gate, lower_bounds, gnorm)


def _post_kernel(x_ref, mix_ref, qm_ref, kv_ref, wo_ref, g_ref, b_ref, o_ref):
    qm = qm_ref[...]
    kmem = kv_ref[0, :, :W_MEM]
    vmem = kv_ref[0, :, W_MEM:]
    lane = lax.broadcasted_iota(jnp.int32, (1, W_MEM), 1)
    mem_out = jnp.zeros(qm.shape, F32)
    for h in range(MEM_HEADS):
        in_head = (lane >= h * MEM_HD) & (lane < (h + 1) * MEM_HD)
        qh = jnp.where(in_head, qm, jnp.zeros_like(qm))
        s = lax.dot_general(qh, kmem, _NT, preferred_element_type=F32)
        p = jnp.exp(s - jnp.max(s, axis=-1, keepdims=True))
        p = p / jnp.sum(p, axis=-1, keepdims=True)
        oh = jnp.dot(p.astype(BF16), vmem, preferred_element_type=F32)
        mem_out = jnp.where(in_head, oh, mem_out)
    y = (jnp.dot(mix_ref[...], wo_ref[:W_SB, :], preferred_element_type=F32)
         + jnp.dot(mem_out.astype(BF16), wo_ref[W_SB:, :], preferred_element_type=F32))
    o_ref[...] = _layer_norm(DN_ALPHA * x_ref[...] + y, g_ref[...], b_ref[...])


def _post(x2d, mix2d, qm2d, kv, w_out, g, b, seq, tm):
    t, d = x2d.shape
    assert seq % tm == 0
    per_b = seq // tm
    return pl.pallas_call(
        _post_kernel,
        grid=(t // tm,),
        in_specs=[pl.BlockSpec((tm, d), lambda i: (i, 0)),
                  pl.BlockSpec((tm, W_SB), lambda i: (i, 0)),
                  pl.BlockSpec((tm, W_MEM), lambda i: (i, 0)),
                  pl.BlockSpec((1, N_MEM, 2 * W_MEM), lambda i: (i // per_b, 0, 0)),
                  pl.BlockSpec((d, d), lambda i: (0, 0)),
                  pl.BlockSpec((1, d), lambda i: (0, 0)),
                  pl.BlockSpec((1, d), lambda i: (0, 0))],
        out_specs=pl.BlockSpec((tm, d), lambda i: (i, 0)),
        out_shape=jax.ShapeDtypeStruct((t, d), F32),
        compiler_params=pltpu.CompilerParams(
            dimension_semantics=("parallel",), vmem_limit_bytes=VMEM_LIMIT_BYTES),
        name="post_mixer",
    )(x2d, mix2d, qm2d, kv, w_out, g, b)


def _mlp_kernel(x_ref, wu_ref, wd_ref, g_ref, b_ref, o_ref):
    x = x_ref[...]
    xb = x.astype(BF16)
    y = jnp.zeros(x.shape, F32)
    for c0 in range(0, D_FF, FF_CHUNK):
        h = jnp.dot(xb, wu_ref[:, c0:c0 + FF_CHUNK], preferred_element_type=F32)
        h = jnp.maximum(h, 0.0)
        y = y + jnp.dot((h * h).astype(BF16), wd_ref[c0:c0 + FF_CHUNK, :],
                        preferred_element_type=F32)
    o_ref[...] = _layer_norm(DN_ALPHA * x + y, g_ref[...], b_ref[...])


def _mlp(x2d, w_up, w_down, g, b, tm):
    t, d = x2d.shape
    resident = dict(pipeline_mode=pl.Buffered(1))
    return pl.pallas_call(
        _mlp_kernel,
        grid=(t // tm,),
        in_specs=[pl.BlockSpec((tm, d), lambda i: (i, 0)),
                  pl.BlockSpec((d, D_FF), lambda i: (0, 0), **resident),
                  pl.BlockSpec((D_FF, d), lambda i: (0, 0), **resident),
                  pl.BlockSpec((1, d), lambda i: (0, 0)),
                  pl.BlockSpec((1, d), lambda i: (0, 0))],
        out_specs=pl.BlockSpec((tm, d), lambda i: (i, 0)),
        out_shape=jax.ShapeDtypeStruct((t, d), F32),
        compiler_params=pltpu.CompilerParams(
            dimension_semantics=("parallel",), vmem_limit_bytes=VMEM_LIMIT_BYTES),
        name="mlp",
    )(x2d, w_up, w_down, g, b)


def kernel(x, mem, w_in_sb, w_in_hg, w_mem_kv, lower_bounds, hg_norm_g, w_out,
           ln_mix_g, ln_mix_b, w_up, w_down, ln_ffn_g, ln_ffn_b):
    b, s, d = x.shape
    t = b * s
    tm = 512
    xf = x.reshape(t, d)
    mem2d = mem.reshape(b * mem.shape[1], d)
    qk_scale = SB_HD ** -0.5
    mem_scale = MEM_HD ** -0.5
    for layer in range(DEPTH):
        slot = layer // N_MIXERS
        if layer % N_MIXERS == 0:
            q, k, v, qm = _proj(xf, w_in_sb[slot].astype(BF16),
                                (W_SB, W_SB, W_SB, W_MEM), (qk_scale, 1.0, 1.0, mem_scale), tm)
            mix = _sb_attention(q.reshape(b, s, W_SB), k.reshape(b, s, W_SB),
                                v.reshape(b, s, W_SB))
        else:
            q, f, i, gate, qm = _proj(xf, w_in_hg[slot].astype(BF16),
                                      (W_HG, W_HG, W_HG, W_HG, W_MEM),
                                      (1.0, 1.0, 1.0, 1.0, mem_scale), tm)
            mix = _hgrn_mixer(q.reshape(b, s, W_HG), f.reshape(b, s, W_HG),
                              i.reshape(b, s, W_HG), gate.reshape(b, s, W_HG),
                              lower_bounds, hg_norm_g[slot].reshape(1, W_HG), layer)
        (kv,) = _proj(mem2d, w_mem_kv[layer].astype(BF16), (2 * W_MEM,), (1.0,), N_MEM)
        x1 = _post(xf, mix.reshape(t, W_SB), qm, kv.reshape(b, N_MEM, 2 * W_MEM),
                   w_out[layer].astype(BF16), ln_mix_g[layer].reshape(1, d),
                   ln_mix_b[layer].reshape(1, d), s, tm)
        xf = _mlp(x1, w_up[layer].astype(BF16), w_down[layer].astype(BF16),
                  ln_ffn_g[layer].reshape(1, d), ln_ffn_b[layer].reshape(1, d), tm)
    return xf.reshape(b, s, d)
```

```python
import functools

import jax
import jax.numpy as jnp
from jax import lax
from jax.experimental import pallas as pl
from jax.experimental.pallas import tpu as pltpu

D_MODEL = 1024
DEPTH = 2
N_MIXERS = 2
N_MEM = 256
MEM_HEADS = 4
MEM_HD = 64
SB_HEADS = 12
SB_HD = 64
HG_HEADS = 6
HG_DK = 128
HG_DV = 128
D_FF = 4 * D_MODEL
W_SB = SB_HEADS * SB_HD
W_HG = HG_HEADS * HG_DK
W_MEM = MEM_HEADS * MEM_HD
DN_ALPHA = (2 * DEPTH) ** 0.25
LN_EPS = 1e-5
RMS_EPS = 1e-6

F32 = jnp.float32
BF16 = jnp.bfloat16

VMEM_LIMIT_BYTES = 48 * 1024 * 1024

SB_TQ = 256
SB_TK = 256
LOG2E = 1.4426950408889634
SB_ZERO_LOG2 = -151.0
HG_CHUNK = 128
HG_SAFE_RANGE = 80.0
FF_CHUNK = 512

_NT = (((1,), (1,)), ((), ()))
_TN = (((0,), (0,)), ((), ()))


def _split_bf16(a, parts):
    out = []
    rem = a
    for _ in range(parts - 1):
        p = rem.astype(BF16)
        out.append(p)
        rem = rem - p.astype(F32)
    out.append(rem.astype(BF16))
    return out


def _layer_norm(r, g, b):
    mu = jnp.mean(r, axis=-1, keepdims=True)
    c = r - mu
    var = jnp.mean(c * c, axis=-1, keepdims=True)
    return c * lax.rsqrt(var + LN_EPS) * g + b


def _proj_kernel(x_ref, w_ref, *out_refs, widths, scales):
    xb = x_ref[...].astype(BF16)
    off = 0
    for o_ref, wd, sc in zip(out_refs, widths, scales):
        y = jnp.dot(xb, w_ref[:, off:off + wd], preferred_element_type=F32)
        if sc != 1.0:
            y = y * sc
        o_ref[...] = y.astype(o_ref.dtype)
        off += wd


def _proj(x2d, w, widths, scales, tm):
    t, k = x2d.shape
    n = w.shape[1]
    assert sum(widths) == n and t % tm == 0
    return pl.pallas_call(
        functools.partial(_proj_kernel, widths=widths, scales=scales),
        grid=(t // tm,),
        in_specs=[pl.BlockSpec((tm, k), lambda i: (i, 0)),
                  pl.BlockSpec((k, n), lambda i: (0, 0))],
        out_specs=[pl.BlockSpec((tm, wd), lambda i: (i, 0)) for wd in widths],
        out_shape=[jax.ShapeDtypeStruct((t, wd), BF16) for wd in widths],
        compiler_params=pltpu.CompilerParams(
            dimension_semantics=("parallel",), vmem_limit_bytes=VMEM_LIMIT_BYTES),
        name="proj",
    )(x2d, w)


def _sb_logs(qs, kt, mask):
    u = lax.dot_general(qs, kt, _NT, preferred_element_type=F32)
    neg_abs = lax.bitcast_convert_type(
        lax.bitcast_convert_type(u, jnp.uint32) | jnp.uint32(0x80000000), F32)
    t = jnp.log(1.0 + jnp.exp2(neg_abs)) * LOG2E
    log_beta = jnp.minimum(u, 0.0) - t
    log_stay = log_beta - u
    if mask is not None:
        log_stay = jnp.where(mask, log_stay, 0.0)
    return log_beta, log_stay


def _sb_later(log_stay, tri2):
    hi, lo = _split_bf16(log_stay, 2)
    return jnp.dot(jnp.concatenate([hi, lo], axis=1), tri2, preferred_element_type=F32)


def _sb_kernel(q_ref, k_ref, v_ref, o_ref):
    qi = pl.program_id(2)
    q2 = q_ref[0]
    lane = lax.broadcasted_iota(jnp.int32, (1, 2 * SB_HD), 1)
    first = lane < SB_HD
    zero = jnp.zeros_like(q2)
    qs = jnp.concatenate([jnp.where(first, q2, zero), jnp.where(first, zero, q2)], axis=0)

    row = lax.broadcasted_iota(jnp.int32, (SB_TQ, SB_TK), 0)
    col = lax.broadcasted_iota(jnp.int32, (SB_TQ, SB_TK), 1)
    tri = (row > col).astype(BF16)
    tri2 = jnp.concatenate([tri, tri], axis=0)
    strict = jnp.concatenate([col < row, col < row], axis=0)

    def tile(j):
        start = pl.multiple_of(j * SB_TK, SB_TK)
        return k_ref[0, pl.ds(start, SB_TK), :], v_ref[0, pl.ds(start, SB_TK), :]

    def diagonal():
        kt, vt = tile(qi)
        log_beta, log_stay = _sb_logs(qs, kt, strict)
        w = jnp.where(strict, jnp.exp2(log_beta + _sb_later(log_stay, tri2)), 0.0)
        acc = jnp.dot(w.astype(BF16), vt, preferred_element_type=F32)
        return jnp.sum(log_stay, axis=-1, keepdims=True), acc

    def full(j, carry, acc):
        kt, vt = tile(j)
        log_beta, log_stay = _sb_logs(qs, kt, None)
        w = jnp.exp2(log_beta + _sb_later(log_stay, tri2) + carry)
        acc = acc + jnp.dot(w.astype(BF16), vt, preferred_element_type=F32)
        return carry + jnp.sum(log_stay, axis=-1, keepdims=True), acc

    def finish(acc):
        o_ref[0] = jnp.where(first, acc[:SB_TQ], acc[SB_TQ:]).astype(o_ref.dtype)

    @pl.when(qi == 0)
    def _():
        finish(diagonal()[1])

    @pl.when(qi > 0)
    def _():
        carry, acc = diagonal()
        carry, acc = full(qi - 1, carry, acc)

        def cond(c):
            j, _, _, go = c
            return jnp.logical_and(j >= 0, go)

        def body(c):
            j, carry, acc, _ = c
            carry, acc = full(j, carry, acc)
            return j - 1, carry, acc, jnp.max(carry) > SB_ZERO_LOG2

        _, _, acc, _ = lax.while_loop(
            cond, body, (qi - 2, carry, acc, jnp.max(carry) > SB_ZERO_LOG2))
        finish(acc)


def _sb_attention(q, k, v):
    b, s, w = q.shape
    assert s % SB_TQ == 0 and SB_TQ == SB_TK
    pair = 2 * SB_HD
    return pl.pallas_call(
        _sb_kernel,
        grid=(b, w // pair, s // SB_TQ),
        in_specs=[pl.BlockSpec((1, SB_TQ, pair), lambda bi, hp, qi: (bi, qi, hp)),
                  pl.BlockSpec((1, s, pair), lambda bi, hp, qi: (bi, 0, hp)),
                  pl.BlockSpec((1, s, pair), lambda bi, hp, qi: (bi, 0, hp))],
        out_specs=pl.BlockSpec((1, SB_TQ, pair), lambda bi, hp, qi: (bi, qi, hp)),
        out_shape=jax.ShapeDtypeStruct((b, s, w), BF16),
        compiler_params=pltpu.CompilerParams(
            dimension_semantics=("parallel", "parallel", "arbitrary"),
            vmem_limit_bytes=VMEM_LIMIT_BYTES),
        name="sb_attention",
    )(q, k, v)


def _hgrn_kernel(q_ref, f_ref, i_ref, gate_ref, lbp_ref, gn_ref, o_ref,
                 st_ref, g_scr, k_scr, v_scr, *, layer):
    c = HG_CHUNK

    @pl.when(pl.program_id(1) == 0)
    def _():
        st_ref[...] = jnp.zeros_like(st_ref)

    rows = [lbp_ref[l:l + 1, :] for l in range(DEPTH)]
    mx = functools.reduce(jnp.maximum, rows)
    es = [jnp.exp(r - mx) for r in rows]
    den = functools.reduce(lambda a, b: a + b, es)
    ps = [e / den for e in es]
    cs = ps[0]
    for l in range(1, layer + 1):
        cs = cs + ps[l]
    lb_all = cs - ps[0]

    trow = lax.broadcasted_iota(jnp.int32, (c, c), 0)
    tcol = lax.broadcasted_iota(jnp.int32, (c, c), 1)
    tril = tcol <= trow
    tri_incl = tril.astype(BF16)

    g_min = None
    for h in range(HG_HEADS):
        sl = slice(h * HG_DK, (h + 1) * HG_DK)
        lb = lb_all[:, sl]
        z = f_ref[0, :, sl].astype(F32)
        t = jnp.log(1.0 + jnp.exp(-jnp.abs(z)))
        sp_pos = jnp.maximum(z, 0.0) + t
        sp_neg = jnp.maximum(-z, 0.0) + t
        a = jnp.log(lb)
        bb = jnp.log(1.0 - lb) - sp_neg
        log_f = jnp.maximum(a, bb) + jnp.log(1.0 + jnp.exp(-jnp.abs(a - bb)))
        k_scr[h] = (1.0 - lb) * jnp.exp(-sp_pos)
        hi, lo = _split_bf16(log_f, 2)
        g_cum = (jnp.dot(tri_incl, hi, preferred_element_type=F32)
                 + jnp.dot(tri_incl, lo, preferred_element_type=F32))
        g_scr[h] = g_cum
        g_last = g_cum[c - 1:c, :]
        g_min = g_last if g_min is None else jnp.minimum(g_min, g_last)
    fast = jnp.min(g_min) >= -HG_SAFE_RANGE

    def head(h, factorised):
        sl = slice(h * HG_DK, (h + 1) * HG_DK)
        q = q_ref[0, :, sl].astype(F32)
        v = i_ref[0, :, sl]
        g_cum = g_scr[h]
        kk = k_scr[h]
        g_last = g_cum[c - 1:c, :]
        qg = (q * jnp.exp(g_cum)).astype(BF16)
        if factorised:
            ke = (kk * jnp.exp(-g_cum)).astype(BF16)
            sc = lax.dot_general(qg, ke, _NT, preferred_element_type=F32)
            sc = jnp.where(tril, sc, 0.0)
            o = jnp.dot(sc.astype(BF16), v, preferred_element_type=F32)
        else:
            v_scr[...] = v.astype(F32)
            row_id = lax.broadcasted_iota(jnp.int32, (c, HG_DK), 0)

            def step(s, acc):
                gs = g_scr[h, pl.ds(s, 1), :]
                dec = jnp.where(row_id >= s, jnp.exp(jnp.minimum(g_cum - gs, 0.0)), 0.0)
                sc = jnp.sum(q * k_scr[h, pl.ds(s, 1), :] * dec, axis=-1, keepdims=True)
                return acc + sc * v_scr[pl.ds(s, 1), :]

            o = lax.fori_loop(0, c, step, jnp.zeros((c, HG_DV), F32))

        st = st_ref[h]
        o = o + lax.dot_general(qg, st.astype(BF16), _NT, preferred_element_type=F32)
        kd = (kk * jnp.exp(g_last - g_cum)).astype(BF16)
        st_ref[h] = st * jnp.exp(g_last) + lax.dot_general(v, kd, _TN, preferred_element_type=F32)

        o = o * lax.rsqrt(jnp.mean(o * o, axis=-1, keepdims=True) + RMS_EPS)
        o = o * gn_ref[:, sl]
        gt = gate_ref[0, :, sl].astype(F32)
        o_ref[0, :, sl] = (o * (gt / (1.0 + jnp.exp(-gt)))).astype(o_ref.dtype)

    @pl.when(fast)
    def _():
        for h in range(HG_HEADS):
            head(h, True)

    @pl.when(jnp.logical_not(fast))
    def _():
        for h in range(HG_HEADS):
            head(h, False)


def _hgrn_mixer(q, f, i, gate, lower_bounds, gnorm, layer):
    b, s, w = q.shape
    c = HG_CHUNK
    assert s % c == 0
    tok = pl.BlockSpec((1, c, w), lambda bi, ci: (bi, ci, 0))
    return pl.pallas_call(
        functools.partial(_hgrn_kernel, layer=layer),
        grid=(b, s // c),
        in_specs=[tok, tok, tok, tok,
                  pl.BlockSpec((DEPTH, w), lambda bi, ci: (0, 0)),
                  pl.BlockSpec((1, w), lambda bi, ci: (0, 0))],
        out_specs=tok,
        out_shape=jax.ShapeDtypeStruct((b, s, w), BF16),
        scratch_shapes=[pltpu.VMEM((HG_HEADS, HG_DV, HG_DK), F32),
                        pltpu.VMEM((HG_HEADS, c, HG_DK), F32),
                        pltpu.VMEM((HG_HEADS, c, HG_DK), F32),
                        pltpu.VMEM((c, HG_DV), F32)],
        compiler_params=pltpu.CompilerParams(
            dimension_semantics=("parallel", "arbitrary"),
            vmem_limit_bytes=VMEM_LIMIT_BYTES),
        name="hgrn2",
    )(q, f, i, gate, lower_bounds, gnorm)


def _post_kernel(x_ref, mix_ref, qm_ref, kv_ref, wo_ref, g_ref, b_ref, o_ref):
    qm = qm_ref[...]
    kmem = kv_ref[0, :, :W_MEM]
    vmem = kv_ref[0, :, W_MEM:]
    lane = lax.broadcasted_iota(jnp.int32, (1, W_MEM), 1)
    mem_out = jnp.zeros(qm.shape, F32)
    for h in range(MEM_HEADS):
        in_head = (lane >= h * MEM_HD) & (lane < (h + 1) * MEM_HD)
        qh = jnp.where(in_head, qm, jnp.zeros_like(qm))
        s = lax.dot_general(qh, kmem, _NT, preferred_element_type=F32)
        p = jnp.exp(s - jnp.max(s, axis=-1, keepdims=True))
        p = p / jnp.sum(p, axis=-1, keepdims=True)
        oh = jnp.dot(p.astype(BF16), vmem, preferred_element_type=F32)
        mem_out = jnp.where(in_head, oh, mem_out)
    y = (jnp.dot(mix_ref[...], wo_ref[:W_SB, :], preferred_element_type=F32)
         + jnp.dot(mem_out.astype(BF16), wo_ref[W_SB:, :], preferred_element_type=F32))
    o_ref[...] = _layer_norm(DN_ALPHA * x_ref[...] + y, g_ref[...], b_ref[...])


def _post(x2d, mix2d, qm2d, kv, w_out, g, b, seq, tm):
    t, d = x2d.shape
    assert seq % tm == 0
    per_b = seq // tm
    return pl.pallas_call(
        _post_kernel,
        grid=(t // tm,),
        in_specs=[pl.BlockSpec((tm, d), lambda i: (i, 0)),
                  pl.BlockSpec((tm, W_SB), lambda i: (i, 0)),
                  pl.BlockSpec((tm, W_MEM), lambda i: (i, 0)),
                  pl.BlockSpec((1, N_MEM, 2 * W_MEM), lambda i: (i // per_b, 0, 0)),
                  pl.BlockSpec((d, d), lambda i: (0, 0)),
                  pl.BlockSpec((1, d), lambda i: (0, 0)),
                  pl.BlockSpec((1, d), lambda i: (0, 0))],
        out_specs=pl.BlockSpec((tm, d), lambda i: (i, 0)),
        out_shape=jax.ShapeDtypeStruct((t, d), F32),
        compiler_params=pltpu.CompilerParams(
            dimension_semantics=("parallel",), vmem_limit_bytes=VMEM_LIMIT_BYTES),
        name="post_mixer",
    )(x2d, mix2d, qm2d, kv, w_out, g, b)


def _mlp_kernel(x_ref, wu_ref, wd_ref, g_ref, b_ref, o_ref):
    x = x_ref[...]
    xb = x.astype(BF16)
    y = jnp.zeros(x.shape, F32)
    for c0 in range(0, D_FF, FF_CHUNK):
        h = jnp.dot(xb, wu_ref[:, c0:c0 + FF_CHUNK], preferred_element_type=F32)
        h = jnp.maximum(h, 0.0)
        y = y + jnp.dot((h * h).astype(BF16), wd_ref[c0:c0 + FF_CHUNK, :],
                        preferred_element_type=F32)
    o_ref[...] = _layer_norm(DN_ALPHA * x + y, g_ref[...], b_ref[...])


def _mlp(x2d, w_up, w_down, g, b, tm):
    t, d = x2d.shape
    resident = dict(pipeline_mode=pl.Buffered(1))
    return pl.pallas_call(
        _mlp_kernel,
        grid=(t // tm,),
        in_specs=[pl.BlockSpec((tm, d), lambda i: (i, 0)),
                  pl.BlockSpec((d, D_FF), lambda i: (0, 0), **resident),
                  pl.BlockSpec((D_FF, d), lambda i: (0, 0), **resident),
                  pl.BlockSpec((1, d), lambda i: (0, 0)),
                  pl.BlockSpec((1, d), lambda i: (0, 0))],
        out_specs=pl.BlockSpec((tm, d), lambda i: (i, 0)),
        out_shape=jax.ShapeDtypeStruct((t, d), F32),
        compiler_params=pltpu.CompilerParams(
            dimension_semantics=("parallel",), vmem_limit_bytes=VMEM_LIMIT_BYTES),
        name="mlp",
    )(x2d, w_up, w_down, g, b)


def kernel(x, mem, w_in_sb, w_in_hg, w_mem_kv, lower_bounds, hg_norm_g, w_out,
           ln_mix_g, ln_mix_b, w_up, w_down, ln_ffn_g, ln_ffn_b):
    b, s, d = x.shape
    t = b * s
    tm = 512
    xf = x.reshape(t, d)
    mem2d = mem.reshape(b * mem.shape[1], d)
    qk_scale = SB_HD ** -0.5 * LOG2E
    mem_scale = MEM_HD ** -0.5
    for layer in range(DEPTH):
        slot = layer // N_MIXERS
        if layer % N_MIXERS == 0:
            q, k, v, qm = _proj(xf, w_in_sb[slot].astype(BF16),
                                (W_SB, W_SB, W_SB, W_MEM), (qk_scale, 1.0, 1.0, mem_scale), tm)
            mix = _sb_attention(q.reshape(b, s, W_SB), k.reshape(b, s, W_SB),
                                v.reshape(b, s, W_SB))
        else:
            q, f, i, gate, qm = _proj(xf, w_in_hg[slot].astype(BF16),
                                      (W_HG, W_HG, W_HG, W_HG, W_MEM),
                                      (1.0, 1.0, 1.0, 1.0, mem_scale), tm)
            mix = _hgrn_mixer(q.reshape(b, s, W_HG), f.reshape(b, s, W_HG),
                              i.reshape(b, s, W_HG), gate.reshape(b, s, W_HG),
                              lower_bounds, hg_norm_g[slot].reshape(1, W_HG), layer)
        (kv,) = _proj(mem2d, w_mem_kv[layer].astype(BF16), (2 * W_MEM,), (1.0,), N_MEM)
        x1 = _post(xf, mix.reshape(t, W_SB), qm, kv.reshape(b, N_MEM, 2 * W_MEM),
                   w_out[layer].astype(BF16), ln_mix_g[layer].reshape(1, d),
                   ln_mix_b[layer].reshape(1, d), s, tm)
        xf = _mlp(x1, w_up[layer].astype(BF16), w_down[layer].astype(BF16),
                  ln_ffn_g[layer].reshape(1, d), ln_ffn_b[layer].reshape(1, d), tm)
    return xf.reshape(b, s, d)
```

```python
import functools

import jax
import jax.numpy as jnp
from jax import lax
from jax.experimental import pallas as pl
from jax.experimental.pallas import tpu as pltpu

D_MODEL = 1024
DEPTH = 2
N_MIXERS = 2
N_MEM = 256
MEM_HEADS = 4
MEM_HD = 64
SB_HEADS = 12
SB_HD = 64
HG_HEADS = 6
HG_DK = 128
HG_DV = 128
D_FF = 4 * D_MODEL
W_SB = SB_HEADS * SB_HD
W_HG = HG_HEADS * HG_DK
W_MEM = MEM_HEADS * MEM_HD
DN_ALPHA = (2 * DEPTH) ** 0.25
LN_EPS = 1e-5
RMS_EPS = 1e-6

F32 = jnp.float32
BF16 = jnp.bfloat16

VMEM_LIMIT_BYTES = 48 * 1024 * 1024

SB_TQ = 256
SB_TK = 256
LOG2E = 1.4426950408889634
SB_ZERO_LOG2 = -151.0
HG_CHUNK = 128
HG_SAFE_RANGE = 80.0
FF_CHUNK = 512

_NT = (((1,), (1,)), ((), ()))
_TN = (((0,), (0,)), ((), ()))


def _split_bf16(a, parts):
    out = []
    rem = a
    for _ in range(parts - 1):
        p = rem.astype(BF16)
        out.append(p)
        rem = rem - p.astype(F32)
    out.append(rem.astype(BF16))
    return out


def _layer_norm(r, g, b):
    mu = jnp.mean(r, axis=-1, keepdims=True)
    c = r - mu
    var = jnp.mean(c * c, axis=-1, keepdims=True)
    return c * lax.rsqrt(var + LN_EPS) * g + b


def _proj_kernel(x_ref, w_ref, *out_refs, widths, scales):
    xb = x_ref[...].astype(BF16)
    off = 0
    for o_ref, wd, sc in zip(out_refs, widths, scales):
        y = jnp.dot(xb, w_ref[:, off:off + wd], preferred_element_type=F32)
        if sc != 1.0:
            y = y * sc
        o_ref[...] = y.astype(o_ref.dtype)
        off += wd


def _proj(x2d, w, widths, scales, tm):
    t, k = x2d.shape
    n = w.shape[1]
    assert sum(widths) == n and t % tm == 0
    return pl.pallas_call(
        functools.partial(_proj_kernel, widths=widths, scales=scales),
        grid=(t // tm,),
        in_specs=[pl.BlockSpec((tm, k), lambda i: (i, 0)),
                  pl.BlockSpec((k, n), lambda i: (0, 0))],
        out_specs=[pl.BlockSpec((tm, wd), lambda i: (i, 0)) for wd in widths],
        out_shape=[jax.ShapeDtypeStruct((t, wd), BF16) for wd in widths],
        compiler_params=pltpu.CompilerParams(
            dimension_semantics=("parallel",), vmem_limit_bytes=VMEM_LIMIT_BYTES),
        name="proj",
    )(x2d, w)


def _sb_logs(qs, kt, mask):
    u = lax.dot_general(qs, kt, _NT, preferred_element_type=F32)
    t = jnp.log(1.0 + jnp.exp2(-jnp.abs(u))) * LOG2E
    log_beta = jnp.minimum(u, 0.0) - t
    log_stay = log_beta - u
    if mask is not None:
        log_stay = jnp.where(mask, log_stay, 0.0)
    return log_beta, log_stay


def _sb_later(log_stay, tri2):
    hi, lo = _split_bf16(log_stay, 2)
    return jnp.dot(jnp.concatenate([hi, lo], axis=1), tri2, preferred_element_type=F32)


def _sb_kernel(q_ref, k_ref, v_ref, o_ref):
    qi = pl.program_id(2)
    q2 = q_ref[0]
    lane = lax.broadcasted_iota(jnp.int32, (1, 2 * SB_HD), 1)
    first = lane < SB_HD
    zero = jnp.zeros_like(q2)
    qs = jnp.concatenate([jnp.where(first, q2, zero), jnp.where(first, zero, q2)], axis=0)

    row = lax.broadcasted_iota(jnp.int32, (SB_TQ, SB_TK), 0)
    col = lax.broadcasted_iota(jnp.int32, (SB_TQ, SB_TK), 1)
    tri = (row > col).astype(BF16)
    tri2 = jnp.concatenate([tri, tri], axis=0)
    strict = jnp.concatenate([col < row, col < row], axis=0)

    def tile(j):
        start = pl.multiple_of(j * SB_TK, SB_TK)
        return k_ref[0, pl.ds(start, SB_TK), :], v_ref[0, pl.ds(start, SB_TK), :]

    def diagonal():
        kt, vt = tile(qi)
        log_beta, log_stay = _sb_logs(qs, kt, strict)
        w = jnp.where(strict, jnp.exp2(log_beta + _sb_later(log_stay, tri2)), 0.0)
        acc = jnp.dot(w.astype(BF16), vt, preferred_element_type=F32)
        return jnp.sum(log_stay, axis=-1, keepdims=True), acc

    def full(j, carry, acc):
        kt, vt = tile(j)
        log_beta, log_stay = _sb_logs(qs, kt, None)
        w = jnp.exp2(log_beta + _sb_later(log_stay, tri2) + carry)
        acc = acc + jnp.dot(w.astype(BF16), vt, preferred_element_type=F32)
        return carry + jnp.sum(log_stay, axis=-1, keepdims=True), acc

    def finish(acc):
        o_ref[0] = jnp.where(first, acc[:SB_TQ], acc[SB_TQ:]).astype(o_ref.dtype)

    @pl.when(qi == 0)
    def _():
        finish(diagonal()[1])

    @pl.when(qi > 0)
    def _():
        k0, v0 = tile(qi)
        k1, v1 = tile(qi - 1)
        hs = (slice(0, SB_TQ), slice(SB_TQ, 2 * SB_TQ))
        strict1 = col < row
        logs0 = [_sb_logs(qs[h], k0, strict1) for h in hs]
        logs1 = [_sb_logs(qs[h], k1, None) for h in hs]
        later0 = [_sb_later(ls, tri2) for _, ls in logs0]
        later1 = [_sb_later(ls, tri2) for _, ls in logs1]
        carry0 = [jnp.sum(ls, axis=-1, keepdims=True) for _, ls in logs0]
        w0 = [jnp.where(strict1, jnp.exp2(logs0[i][0] + later0[i]), 0.0) for i in range(2)]
        w1 = [jnp.exp2(logs1[i][0] + later1[i] + carry0[i]) for i in range(2)]
        accs = [jnp.dot(w0[i].astype(BF16), v0, preferred_element_type=F32)
                + jnp.dot(w1[i].astype(BF16), v1, preferred_element_type=F32) for i in range(2)]
        acc = jnp.concatenate(accs, axis=0)
        carry = jnp.concatenate(
            [carry0[i] + jnp.sum(logs1[i][1], axis=-1, keepdims=True) for i in range(2)], axis=0)

        def cond(c):
            j, _, _, go = c
            return jnp.logical_and(j >= 0, go)

        def body(c):
            j, carry, acc, _ = c
            carry, acc = full(j, carry, acc)
            return j - 1, carry, acc, jnp.max(carry) > SB_ZERO_LOG2

        _, _, acc, _ = lax.while_loop(
            cond, body, (qi - 2, carry, acc, jnp.max(carry) > SB_ZERO_LOG2))
        finish(acc)


def _sb_attention(q, k, v):
    b, s, w = q.shape
    assert s % SB_TQ == 0 and SB_TQ == SB_TK
    pair = 2 * SB_HD
    return pl.pallas_call(
        _sb_kernel,
        grid=(b, w // pair, s // SB_TQ),
        in_specs=[pl.BlockSpec((1, SB_TQ, pair), lambda bi, hp, qi: (bi, qi, hp)),
                  pl.BlockSpec((1, s, pair), lambda bi, hp, qi: (bi, 0, hp)),
                  pl.BlockSpec((1, s, pair), lambda bi, hp, qi: (bi, 0, hp))],
        out_specs=pl.BlockSpec((1, SB_TQ, pair), lambda bi, hp, qi: (bi, qi, hp)),
        out_shape=jax.ShapeDtypeStruct((b, s, w), BF16),
        compiler_params=pltpu.CompilerParams(
            dimension_semantics=("parallel", "parallel", "arbitrary"),
            vmem_limit_bytes=VMEM_LIMIT_BYTES),
        name="sb_attention",
    )(q, k, v)


def _hgrn_kernel(q_ref, f_ref, i_ref, gate_ref, lbp_ref, gn_ref, o_ref,
                 st_ref, g_scr, k_scr, v_scr, *, layer):
    c = HG_CHUNK

    @pl.when(pl.program_id(1) == 0)
    def _():
        st_ref[...] = jnp.zeros_like(st_ref)

    rows = [lbp_ref[l:l + 1, :] for l in range(DEPTH)]
    mx = functools.reduce(jnp.maximum, rows)
    es = [jnp.exp(r - mx) for r in rows]
    den = functools.reduce(lambda a, b: a + b, es)
    ps = [e / den for e in es]
    cs = ps[0]
    for l in range(1, layer + 1):
        cs = cs + ps[l]
    lb_all = cs - ps[0]

    trow = lax.broadcasted_iota(jnp.int32, (c, c), 0)
    tcol = lax.broadcasted_iota(jnp.int32, (c, c), 1)
    tril = tcol <= trow
    tri_incl = tril.astype(BF16)

    cur = pl.program_id(1) % 2
    nxt = 1 - cur

    heads = range(HG_HEADS)
    sls = [slice(h * HG_DK, (h + 1) * HG_DK) for h in heads]

    def gate_logs(h):
        lb = lb_all[:, sls[h]]
        z = f_ref[0, :, sls[h]].astype(F32)
        sp_pos = jnp.maximum(z, 0.0) + jnp.log(1.0 + jnp.exp(-jnp.abs(z)))
        a = jnp.log(lb)
        bb = jnp.log(1.0 - lb) - (sp_pos - z)
        log_f = jnp.maximum(a, bb) + jnp.log(1.0 + jnp.exp(-jnp.abs(a - bb)))
        return log_f, (1.0 - lb) * jnp.exp(-sp_pos)

    def cumulate(log_f):
        hi, lo = _split_bf16(log_f, 2)
        return (jnp.dot(tri_incl, hi, preferred_element_type=F32)
                + jnp.dot(tri_incl, lo, preferred_element_type=F32))

    def finish(h, o, qg, kk, g_cum):
        v = i_ref[0, :, sls[h]]
        g_last = g_cum[c - 1:c, :]
        st = st_ref[cur, h]
        o = o + lax.dot_general(qg, st.astype(BF16), _NT, preferred_element_type=F32)
        kd = (kk * jnp.exp(g_last - g_cum)).astype(BF16)
        st_ref[nxt, h] = (st * jnp.exp(g_last)
                          + lax.dot_general(v, kd, _TN, preferred_element_type=F32))
        o = o * lax.rsqrt(jnp.mean(o * o, axis=-1, keepdims=True) + RMS_EPS)
        o = o * gn_ref[:, sls[h]]
        gt = gate_ref[0, :, sls[h]].astype(F32)
        o_ref[0, :, sls[h]] = (o * (gt / (1.0 + jnp.exp(-gt)))).astype(o_ref.dtype)

    logs = [gate_logs(h) for h in heads]
    g_cums = [cumulate(log_f) for log_f, _ in logs]
    kks = [kk for _, kk in logs]
    for h in heads:
        g_scr[h] = g_cums[h]
        k_scr[h] = kks[h]
    qgs = [(q_ref[0, :, sls[h]].astype(F32) * jnp.exp(g_cums[h])).astype(BF16) for h in heads]
    kes = [(kks[h] * jnp.exp(-g_cums[h])).astype(BF16) for h in heads]
    scs = [jnp.where(tril, lax.dot_general(qgs[h], kes[h], _NT, preferred_element_type=F32), 0.0)
           for h in heads]
    o_intra = [jnp.dot(scs[h].astype(BF16), i_ref[0, :, sls[h]], preferred_element_type=F32)
               for h in heads]
    for h in heads:
        finish(h, o_intra[h], qgs[h], kks[h], g_cums[h])

    g_min = functools.reduce(jnp.minimum, [g[c - 1:c, :] for g in g_cums])

    @pl.when(jnp.logical_not(jnp.min(g_min) >= -HG_SAFE_RANGE))
    def _():
        row_id = lax.broadcasted_iota(jnp.int32, (c, HG_DK), 0)
        for h in heads:
            q = q_ref[0, :, sls[h]].astype(F32)
            g_cum = g_scr[h]
            v_scr[...] = i_ref[0, :, sls[h]].astype(F32)

            def step(s, acc, h=h, q=q, g_cum=g_cum):
                gs = g_scr[h, pl.ds(s, 1), :]
                dec = jnp.where(row_id >= s, jnp.exp(jnp.minimum(g_cum - gs, 0.0)), 0.0)
                sc = jnp.sum(q * k_scr[h, pl.ds(s, 1), :] * dec, axis=-1, keepdims=True)
                return acc + sc * v_scr[pl.ds(s, 1), :]

            o = lax.fori_loop(0, c, step, jnp.zeros((c, HG_DV), F32))
            finish(h, o, (q * jnp.exp(g_cum)).astype(BF16), k_scr[h], g_cum)


def _hgrn_mixer(q, f, i, gate, lower_bounds, gnorm, layer):
    b, s, w = q.shape
    c = HG_CHUNK
    assert s % c == 0
    tok = pl.BlockSpec((1, c, w), lambda bi, ci: (bi, ci, 0))
    return pl.pallas_call(
        functools.partial(_hgrn_kernel, layer=layer),
        grid=(b, s // c),
        in_specs=[tok, tok, tok, tok,
                  pl.BlockSpec((DEPTH, w), lambda bi, ci: (0, 0)),
                  pl.BlockSpec((1, w), lambda bi, ci: (0, 0))],
        out_specs=tok,
        out_shape=jax.ShapeDtypeStruct((b, s, w), BF16),
        scratch_shapes=[pltpu.VMEM((2, HG_HEADS, HG_DV, HG_DK), F32),
                        pltpu.VMEM((HG_HEADS, c, HG_DK), F32),
                        pltpu.VMEM((HG_HEADS, c, HG_DK), F32),
                        pltpu.VMEM((c, HG_DV), F32)],
        compiler_params=pltpu.CompilerParams(
            dimension_semantics=("parallel", "arbitrary"),
            vmem_limit_bytes=VMEM_LIMIT_BYTES),
        name="hgrn2",
    )(q, f, i, gate, lower_bounds, gnorm)


def _post_kernel(x_ref, mix_ref, qm_ref, kv_ref, wo_ref, g_ref, b_ref, o_ref):
    kmem = kv_ref[0, :, :W_MEM]
    vmem = kv_ref[0, :, W_MEM:]
    lane = lax.broadcasted_iota(jnp.int32, (1, W_MEM), 1)
    qm = qm_ref[...]
    mix = mix_ref[...]
    mem_out = jnp.zeros(qm.shape, F32)
    cw = wo_ref.shape[1] // MEM_HEADS
    y_cols = []
    for h in range(MEM_HEADS):
        y_cols.append(jnp.dot(mix, wo_ref[:W_SB, h * cw:(h + 1) * cw],
                              preferred_element_type=F32))
        in_head = (lane >= h * MEM_HD) & (lane < (h + 1) * MEM_HD)
        qh = jnp.where(in_head, qm, jnp.zeros_like(qm))
        s = lax.dot_general(qh, kmem, _NT, preferred_element_type=F32)
        p = jnp.exp(s - jnp.max(s, axis=-1, keepdims=True))
        p = p / jnp.sum(p, axis=-1, keepdims=True)
        oh = jnp.dot(p.astype(BF16), vmem, preferred_element_type=F32)
        mem_out = jnp.where(in_head, oh, mem_out)
    y = (jnp.concatenate(y_cols, axis=1)
         + jnp.dot(mem_out.astype(BF16), wo_ref[W_SB:, :], preferred_element_type=F32))
    o_ref[...] = _layer_norm(DN_ALPHA * x_ref[...] + y, g_ref[...], b_ref[...])


def _post(x2d, mix2d, qm2d, kv, w_out, g, b, seq, tm):
    t, d = x2d.shape
    assert seq % tm == 0
    per_b = seq // tm
    return pl.pallas_call(
        _post_kernel,
        grid=(t // tm,),
        in_specs=[pl.BlockSpec((tm, d), lambda i: (i, 0)),
                  pl.BlockSpec((tm, W_SB), lambda i: (i, 0)),
                  pl.BlockSpec((tm, W_MEM), lambda i: (i, 0)),
                  pl.BlockSpec((1, N_MEM, 2 * W_MEM), lambda i: (i // per_b, 0, 0)),
                  pl.BlockSpec((d, d), lambda i: (0, 0)),
                  pl.BlockSpec((1, d), lambda i: (0, 0)),
                  pl.BlockSpec((1, d), lambda i: (0, 0))],
        out_specs=pl.BlockSpec((tm, d), lambda i: (i, 0)),
        out_shape=jax.ShapeDtypeStruct((t, d), F32),
        compiler_params=pltpu.CompilerParams(
            dimension_semantics=("parallel",), vmem_limit_bytes=VMEM_LIMIT_BYTES),
        name="post_mixer",
    )(x2d, mix2d, qm2d, kv, w_out, g, b)


def _mlp_kernel(x_ref, wu_ref, wd_ref, g_ref, b_ref, o_ref):
    x = x_ref[...]
    xb = x.astype(BF16)
    y = jnp.zeros(x.shape, F32)
    for c0 in range(0, D_FF, FF_CHUNK):
        h = jnp.dot(xb, wu_ref[:, c0:c0 + FF_CHUNK], preferred_element_type=F32)
        h = jnp.maximum(h, 0.0)
        y = y + jnp.dot((h * h).astype(BF16), wd_ref[c0:c0 + FF_CHUNK, :],
                        preferred_element_type=F32)
    o_ref[...] = _layer_norm(DN_ALPHA * x + y, g_ref[...], b_ref[...])


def _mlp(x2d, w_up, w_down, g, b, tm):
    t, d = x2d.shape
    resident = dict(pipeline_mode=pl.Buffered(1))
    return pl.pallas_call(
        _mlp_kernel,
        grid=(t // tm,),
        in_specs=[pl.BlockSpec((tm, d), lambda i: (i, 0)),
                  pl.BlockSpec((d, D_FF), lambda i: (0, 0), **resident),
                  pl.BlockSpec((D_FF, d), lambda i: (0, 0), **resident),
                  pl.BlockSpec((1, d), lambda i: (0, 0)),
                  pl.BlockSpec((1, d), lambda i: (0, 0))],
        out_specs=pl.BlockSpec((tm, d), lambda i: (i, 0)),
        out_shape=jax.ShapeDtypeStruct((t, d), F32),
        compiler_params=pltpu.CompilerParams(
            dimension_semantics=("parallel",), vmem_limit_bytes=VMEM_LIMIT_BYTES),
        name="mlp",
    )(x2d, w_up, w_down, g, b)


def kernel(x, mem, w_in_sb, w_in_hg, w_mem_kv, lower_bounds, hg_norm_g, w_out,
           ln_mix_g, ln_mix_b, w_up, w_down, ln_ffn_g, ln_ffn_b):
    b, s, d = x.shape
    t = b * s
    tm = 512
    xf = x.reshape(t, d)
    mem2d = mem.reshape(b * mem.shape[1], d)
    qk_scale = SB_HD ** -0.5 * LOG2E
    mem_scale = MEM_HD ** -0.5
    for layer in range(DEPTH):
        slot = layer // N_MIXERS
        if layer % N_MIXERS == 0:
            q, k, v, qm = _proj(xf, w_in_sb[slot].astype(BF16),
                                (W_SB, W_SB, W_SB, W_MEM), (qk_scale, 1.0, 1.0, mem_scale), tm)
            mix = _sb_attention(q.reshape(b, s, W_SB), k.reshape(b, s, W_SB),
                                v.reshape(b, s, W_SB))
        else:
            q, f, i, gate, qm = _proj(xf, w_in_hg[slot].astype(BF16),
                                      (W_HG, W_HG, W_HG, W_HG, W_MEM),
                                      (1.0, 1.0, 1.0, 1.0, mem_scale), tm)
            mix = _hgrn_mixer(q.reshape(b, s, W_HG), f.reshape(b, s, W_HG),
                              i.reshape(b, s, W_HG), gate.reshape(b, s, W_HG),
                              lower_bounds, hg_norm_g[slot].reshape(1, W_HG), layer)
        (kv,) = _proj(mem2d, w_mem_kv[layer].astype(BF16), (2 * W_MEM,), (1.0,), N_MEM)
        x1 = _post(xf, mix.reshape(t, W_SB), qm, kv.reshape(b, N_MEM, 2 * W_MEM),
                   w_out[layer].astype(BF16), ln_mix_g[layer].reshape(1, d),
                   ln_mix_b[layer].reshape(1, d), s, tm)
        xf = _mlp(x1, w_up[layer].astype(BF16), w_down[layer].astype(BF16),
                  ln_ffn_g[layer].reshape(1, d), ln_ffn_b[layer].reshape(1, d), tm)
    return xf.reshape(b, s, d)
```

```python
import functools

import jax
import jax.numpy as jnp
from jax import lax
from jax.experimental import pallas as pl
from jax.experimental.pallas import tpu as pltpu

D_MODEL = 1024
DEPTH = 2
N_MIXERS = 2
N_MEM = 256
MEM_HEADS = 4
MEM_HD = 64
SB_HEADS = 12
SB_HD = 64
HG_HEADS = 6
HG_DK = 128
HG_DV = 128
D_FF = 4 * D_MODEL
W_SB = SB_HEADS * SB_HD
W_HG = HG_HEADS * HG_DK
W_MEM = MEM_HEADS * MEM_HD
DN_ALPHA = (2 * DEPTH) ** 0.25
LN_EPS = 1e-5
RMS_EPS = 1e-6

F32 = jnp.float32
BF16 = jnp.bfloat16

VMEM_LIMIT_BYTES = 48 * 1024 * 1024

SB_TQ = 256
SB_TK = 256
SB_QB = 2
LOG2E = 1.4426950408889634
SB_ZERO_LOG2 = -151.0
HG_CHUNK = 128
HG_SAFE_RANGE = 80.0
FF_CHUNK = 512

_NT = (((1,), (1,)), ((), ()))
_TN = (((0,), (0,)), ((), ()))


def _split_bf16(a, parts):
    out = []
    rem = a
    for _ in range(parts - 1):
        p = rem.astype(BF16)
        out.append(p)
        rem = rem - p.astype(F32)
    out.append(rem.astype(BF16))
    return out


def _layer_norm(r, g, b):
    mu = jnp.mean(r, axis=-1, keepdims=True)
    c = r - mu
    var = jnp.mean(c * c, axis=-1, keepdims=True)
    return c * lax.rsqrt(var + LN_EPS) * g + b


def _proj_kernel(x_ref, w_ref, *out_refs, widths, scales):
    xb = x_ref[...].astype(BF16)
    off = 0
    for o_ref, wd, sc in zip(out_refs, widths, scales):
        y = jnp.dot(xb, w_ref[:, off:off + wd], preferred_element_type=F32)
        if sc != 1.0:
            y = y * sc
        o_ref[...] = y.astype(o_ref.dtype)
        off += wd


def _proj(x2d, w, widths, scales, tm):
    t, k = x2d.shape
    n = w.shape[1]
    assert sum(widths) == n and t % tm == 0
    return pl.pallas_call(
        functools.partial(_proj_kernel, widths=widths, scales=scales),
        grid=(t // tm,),
        in_specs=[pl.BlockSpec((tm, k), lambda i: (i, 0)),
                  pl.BlockSpec((k, n), lambda i: (0, 0))],
        out_specs=[pl.BlockSpec((tm, wd), lambda i: (i, 0)) for wd in widths],
        out_shape=[jax.ShapeDtypeStruct((t, wd), BF16) for wd in widths],
        compiler_params=pltpu.CompilerParams(
            dimension_semantics=("parallel",), vmem_limit_bytes=VMEM_LIMIT_BYTES),
        name="proj",
    )(x2d, w)


def _sb_logs(qs, kt, mask):
    u = lax.dot_general(qs, kt, _NT, preferred_element_type=F32)
    t = jnp.log(1.0 + jnp.exp2(-jnp.abs(u))) * LOG2E
    log_beta = jnp.minimum(u, 0.0) - t
    log_stay = log_beta - u
    if mask is not None:
        log_stay = jnp.where(mask, log_stay, 0.0)
    return log_beta, log_stay


def _sb_later(log_stay, tri2):
    hi, lo = _split_bf16(log_stay, 2)
    return jnp.dot(jnp.concatenate([hi, lo], axis=1), tri2, preferred_element_type=F32)


def _sb_kernel(q_ref, k_ref, v_ref, o_ref):
    first_tile = pl.program_id(2) * SB_QB
    lane = lax.broadcasted_iota(jnp.int32, (1, 2 * SB_HD), 1)
    first = lane < SB_HD
    row = lax.broadcasted_iota(jnp.int32, (SB_TQ, SB_TK), 0)
    col = lax.broadcasted_iota(jnp.int32, (SB_TQ, SB_TK), 1)
    tri = (row > col).astype(BF16)
    tri2 = jnp.concatenate([tri, tri], axis=0)
    strict = col < row
    has_prev = jnp.full((SB_TQ, SB_TK), first_tile, jnp.int32) > 0

    def tile(j):
        start = pl.multiple_of(j * SB_TK, SB_TK)
        return k_ref[0, pl.ds(start, SB_TK), :], v_ref[0, pl.ds(start, SB_TK), :]

    qh = []
    for n in range(SB_QB):
        q2 = q_ref[0, n * SB_TQ:(n + 1) * SB_TQ, :]
        zero = jnp.zeros_like(q2)
        qh.append((jnp.where(first, q2, zero), jnp.where(first, zero, q2)))

    kv = [tile(jnp.maximum(first_tile - 1, 0))] + [tile(first_tile + n) for n in range(SB_QB)]
    chains = [(n, h) for n in range(SB_QB) for h in range(2)]
    prev_mask = [has_prev if n == 0 else None for n, _ in chains]
    logs_d = [_sb_logs(qh[n][h], kv[n + 1][0], strict) for n, h in chains]
    logs_p = [_sb_logs(qh[n][h], kv[n][0], m) for (n, h), m in zip(chains, prev_mask)]
    later_d = [_sb_later(ls, tri2) for _, ls in logs_d]
    later_p = [_sb_later(ls, tri2) for _, ls in logs_p]
    carry_d = [jnp.sum(ls, axis=-1, keepdims=True) for _, ls in logs_d]
    w_d = [jnp.where(strict, jnp.exp2(lb + lt), 0.0) for (lb, _), lt in zip(logs_d, later_d)]
    w_p = [jnp.exp2(lb + lt + cd) for (lb, _), lt, cd in zip(logs_p, later_p, carry_d)]
    w_p = [w if m is None else jnp.where(m, w, 0.0) for w, m in zip(w_p, prev_mask)]
    accs = [jnp.dot(wd.astype(BF16), kv[n + 1][1], preferred_element_type=F32)
            + jnp.dot(wp.astype(BF16), kv[n][1], preferred_element_type=F32)
            for (n, _), wd, wp in zip(chains, w_d, w_p)]
    carries = [cd + jnp.sum(ls, axis=-1, keepdims=True) for cd, (_, ls) in zip(carry_d, logs_p)]

    stacked = [jnp.concatenate(carries[2 * n:2 * n + 2], axis=0) for n in range(SB_QB)]
    go = [jnp.max(c) > SB_ZERO_LOG2 for c in stacked]
    for n in range(SB_QB):
        qs = jnp.concatenate(qh[n], axis=0)
        carry = stacked[n]
        acc = jnp.concatenate(accs[2 * n:2 * n + 2], axis=0)

        def cond(c):
            j, _, _, go = c
            return jnp.logical_and(j >= 0, go)

        def body(c, qs=qs):
            j, carry, acc, _ = c
            kt, vt = tile(j)
            log_beta, log_stay = _sb_logs(qs, kt, None)
            w = jnp.exp2(log_beta + _sb_later(log_stay, tri2) + carry)
            acc = acc + jnp.dot(w.astype(BF16), vt, preferred_element_type=F32)
            carry = carry + jnp.sum(log_stay, axis=-1, keepdims=True)
            return j - 1, carry, acc, jnp.max(carry) > SB_ZERO_LOG2

        _, _, acc, _ = lax.while_loop(
            cond, body, (first_tile + n - 2, carry, acc, go[n]))
        o_ref[0, n * SB_TQ:(n + 1) * SB_TQ, :] = jnp.where(
            first, acc[:SB_TQ], acc[SB_TQ:]).astype(o_ref.dtype)


def _sb_attention(q, k, v):
    b, s, w = q.shape
    tq = SB_QB * SB_TQ
    assert s % tq == 0 and SB_TQ == SB_TK
    pair = 2 * SB_HD
    return pl.pallas_call(
        _sb_kernel,
        grid=(b, w // pair, s // tq),
        in_specs=[pl.BlockSpec((1, tq, pair), lambda bi, hp, qi: (bi, qi, hp)),
                  pl.BlockSpec((1, s, pair), lambda bi, hp, qi: (bi, 0, hp)),
                  pl.BlockSpec((1, s, pair), lambda bi, hp, qi: (bi, 0, hp))],
        out_specs=pl.BlockSpec((1, tq, pair), lambda bi, hp, qi: (bi, qi, hp)),
        out_shape=jax.ShapeDtypeStruct((b, s, w), BF16),
        compiler_params=pltpu.CompilerParams(
            dimension_semantics=("parallel", "parallel", "arbitrary"),
            vmem_limit_bytes=VMEM_LIMIT_BYTES),
        name="sb_attention",
    )(q, k, v)


def _hgrn_kernel(q_ref, f_ref, i_ref, gate_ref, lbp_ref, gn_ref, o_ref,
                 st_ref, g_scr, k_scr, v_scr, *, layer):
    c = HG_CHUNK

    @pl.when(pl.program_id(1) == 0)
    def _():
        st_ref[...] = jnp.zeros_like(st_ref)

    rows = [lbp_ref[l:l + 1, :] for l in range(DEPTH)]
    mx = functools.reduce(jnp.maximum, rows)
    es = [jnp.exp(r - mx) for r in rows]
    den = functools.reduce(lambda a, b: a + b, es)
    ps = [e / den for e in es]
    cs = ps[0]
    for l in range(1, layer + 1):
        cs = cs + ps[l]
    lb_all = cs - ps[0]

    trow = lax.broadcasted_iota(jnp.int32, (c, c), 0)
    tcol = lax.broadcasted_iota(jnp.int32, (c, c), 1)
    tril = tcol <= trow
    tri_incl = tril.astype(BF16)

    cur = pl.program_id(1) % 2
    nxt = 1 - cur

    heads = range(HG_HEADS)
    sls = [slice(h * HG_DK, (h + 1) * HG_DK) for h in heads]

    def gate_logs(h):
        lb = lb_all[:, sls[h]]
        z = f_ref[0, :, sls[h]].astype(F32)
        sp_pos = jnp.maximum(z, 0.0) + jnp.log(1.0 + jnp.exp(-jnp.abs(z)))
        a = jnp.log(lb)
        bb = jnp.log(1.0 - lb) - (sp_pos - z)
        log_f = jnp.maximum(a, bb) + jnp.log(1.0 + jnp.exp(-jnp.abs(a - bb)))
        return log_f, (1.0 - lb) * jnp.exp(-sp_pos)

    def cumulate(log_f):
        hi, lo = _split_bf16(log_f, 2)
        return (jnp.dot(tri_incl, hi, preferred_element_type=F32)
                + jnp.dot(tri_incl, lo, preferred_element_type=F32))

    def finish(h, o, qg, kd, g_last):
        v = i_ref[0, :, sls[h]]
        st = st_ref[cur, h]
        o = o + lax.dot_general(qg, st.astype(BF16), _NT, preferred_element_type=F32)
        st_ref[nxt, h] = (st * jnp.exp(g_last)
                          + lax.dot_general(v, kd.astype(BF16), _TN, preferred_element_type=F32))
        o = o * lax.rsqrt(jnp.mean(o * o, axis=-1, keepdims=True) + RMS_EPS)
        o = o * gn_ref[:, sls[h]]
        gt = gate_ref[0, :, sls[h]].astype(F32)
        o_ref[0, :, sls[h]] = (o * (gt / (1.0 + jnp.exp(-gt)))).astype(o_ref.dtype)

    logs = [gate_logs(h) for h in heads]
    g_cums = [cumulate(log_f) for log_f, _ in logs]
    kks = [kk for _, kk in logs]
    for h in heads:
        g_scr[h] = g_cums[h]
        k_scr[h] = kks[h]
    qgs = [(q_ref[0, :, sls[h]].astype(F32) * jnp.exp(g_cums[h])).astype(BF16) for h in heads]
    kes = [kks[h] * jnp.exp(-g_cums[h]) for h in heads]
    scs = [jnp.where(tril, lax.dot_general(qgs[h], kes[h].astype(BF16), _NT,
                                           preferred_element_type=F32), 0.0) for h in heads]
    o_intra = [jnp.dot(scs[h].astype(BF16), i_ref[0, :, sls[h]], preferred_element_type=F32)
               for h in heads]
    for h in heads:
        g_last = g_cums[h][c - 1:c, :]
        finish(h, o_intra[h], qgs[h], kes[h] * jnp.exp(g_last), g_last)

    g_min = functools.reduce(jnp.minimum, [g[c - 1:c, :] for g in g_cums])

    @pl.when(jnp.logical_not(jnp.min(g_min) >= -HG_SAFE_RANGE))
    def _():
        row_id = lax.broadcasted_iota(jnp.int32, (c, HG_DK), 0)
        for h in heads:
            q = q_ref[0, :, sls[h]].astype(F32)
            g_cum = g_scr[h]
            v_scr[...] = i_ref[0, :, sls[h]].astype(F32)

            def step(s, acc, h=h, q=q, g_cum=g_cum):
                gs = g_scr[h, pl.ds(s, 1), :]
                dec = jnp.where(row_id >= s, jnp.exp(jnp.minimum(g_cum - gs, 0.0)), 0.0)
                sc = jnp.sum(q * k_scr[h, pl.ds(s, 1), :] * dec, axis=-1, keepdims=True)
                return acc + sc * v_scr[pl.ds(s, 1), :]

            o = lax.fori_loop(0, c, step, jnp.zeros((c, HG_DV), F32))
            g_last = g_cum[c - 1:c, :]
            finish(h, o, (q * jnp.exp(g_cum)).astype(BF16),
                   k_scr[h] * jnp.exp(g_last - g_cum), g_last)


def _hgrn_mixer(q, f, i, gate, lower_bounds, gnorm, layer):
    b, s, w = q.shape
    c = HG_CHUNK
    assert s % c == 0
    tok = pl.BlockSpec((1, c, w), lambda bi, ci: (bi, ci, 0))
    return pl.pallas_call(
        functools.partial(_hgrn_kernel, layer=layer),
        grid=(b, s // c),
        in_specs=[tok, tok, tok, tok,
                  pl.BlockSpec((DEPTH, w), lambda bi, ci: (0, 0)),
                  pl.BlockSpec((1, w), lambda bi, ci: (0, 0))],
        out_specs=tok,
        out_shape=jax.ShapeDtypeStruct((b, s, w), BF16),
        scratch_shapes=[pltpu.VMEM((2, HG_HEADS, HG_DV, HG_DK), F32),
                        pltpu.VMEM((HG_HEADS, c, HG_DK), F32),
                        pltpu.VMEM((HG_HEADS, c, HG_DK), F32),
                        pltpu.VMEM((c, HG_DV), F32)],
        compiler_params=pltpu.CompilerParams(
            dimension_semantics=("parallel", "arbitrary"),
            vmem_limit_bytes=VMEM_LIMIT_BYTES),
        name="hgrn2",
    )(q, f, i, gate, lower_bounds, gnorm)


def _post_kernel(x_ref, mix_ref, qm_ref, kv_ref, wo_ref, g_ref, b_ref, o_ref):
    kmem = kv_ref[0, :, :W_MEM]
    vmem = kv_ref[0, :, W_MEM:]
    lane = lax.broadcasted_iota(jnp.int32, (1, W_MEM), 1)
    qm = qm_ref[...]
    mix = mix_ref[...]
    mem_out = jnp.zeros(qm.shape, F32)
    cw = wo_ref.shape[1] // MEM_HEADS
    y_cols = []
    for h in range(MEM_HEADS):
        y_cols.append(jnp.dot(mix, wo_ref[:W_SB, h * cw:(h + 1) * cw],
                              preferred_element_type=F32))
        in_head = (lane >= h * MEM_HD) & (lane < (h + 1) * MEM_HD)
        qh = jnp.where(in_head, qm, jnp.zeros_like(qm))
        s = lax.dot_general(qh, kmem, _NT, preferred_element_type=F32)
        p = jnp.exp(s - jnp.max(s, axis=-1, keepdims=True))
        p = p / jnp.sum(p, axis=-1, keepdims=True)
        oh = jnp.dot(p.astype(BF16), vmem, preferred_element_type=F32)
        mem_out = jnp.where(in_head, oh, mem_out)
    y = (jnp.concatenate(y_cols, axis=1)
         + jnp.dot(mem_out.astype(BF16), wo_ref[W_SB:, :], preferred_element_type=F32))
    o_ref[...] = _layer_norm(DN_ALPHA * x_ref[...] + y, g_ref[...], b_ref[...])


def _post(x2d, mix2d, qm2d, kv, w_out, g, b, seq, tm):
    t, d = x2d.shape
    assert seq % tm == 0
    per_b = seq // tm
    return pl.pallas_call(
        _post_kernel,
        grid=(t // tm,),
        in_specs=[pl.BlockSpec((tm, d), lambda i: (i, 0)),
                  pl.BlockSpec((tm, W_SB), lambda i: (i, 0)),
                  pl.BlockSpec((tm, W_MEM), lambda i: (i, 0)),
                  pl.BlockSpec((1, N_MEM, 2 * W_MEM), lambda i: (i // per_b, 0, 0)),
                  pl.BlockSpec((d, d), lambda i: (0, 0)),
                  pl.BlockSpec((1, d), lambda i: (0, 0)),
                  pl.BlockSpec((1, d), lambda i: (0, 0))],
        out_specs=pl.BlockSpec((tm, d), lambda i: (i, 0)),
        out_shape=jax.ShapeDtypeStruct((t, d), F32),
        compiler_params=pltpu.CompilerParams(
            dimension_semantics=("parallel",), vmem_limit_bytes=VMEM_LIMIT_BYTES),
        name="post_mixer",
    )(x2d, mix2d, qm2d, kv, w_out, g, b)


def _mlp_kernel(x_ref, wu_ref, wd_ref, g_ref, b_ref, o_ref):
    x = x_ref[...]
    xb = x.astype(BF16)
    y = jnp.zeros(x.shape, F32)
    for c0 in range(0, D_FF, FF_CHUNK):
        h = jnp.dot(xb, wu_ref[:, c0:c0 + FF_CHUNK], preferred_element_type=F32)
        h = jnp.maximum(h, 0.0)
        y = y + jnp.dot((h * h).astype(BF16), wd_ref[c0:c0 + FF_CHUNK, :],
                        preferred_element_type=F32)
    o_ref[...] = _layer_norm(DN_ALPHA * x + y, g_ref[...], b_ref[...])


def _mlp(x2d, w_up, w_down, g, b, tm):
    t, d = x2d.shape
    resident = dict(pipeline_mode=pl.Buffered(1))
    return pl.pallas_call(
        _mlp_kernel,
        grid=(t // tm,),
        in_specs=[pl.BlockSpec((tm, d), lambda i: (i, 0)),
                  pl.BlockSpec((d, D_FF), lambda i: (0, 0), **resident),
                  pl.BlockSpec((D_FF, d), lambda i: (0, 0), **resident),
                  pl.BlockSpec((1, d), lambda i: (0, 0)),
                  pl.BlockSpec((1, d), lambda i: (0, 0))],
        out_specs=pl.BlockSpec((tm, d), lambda i: (i, 0)),
        out_shape=jax.ShapeDtypeStruct((t, d), F32),
        compiler_params=pltpu.CompilerParams(
            dimension_semantics=("parallel",), vmem_limit_bytes=VMEM_LIMIT_BYTES),
        name="mlp",
    )(x2d, w_up, w_down, g, b)


def kernel(x, mem, w_in_sb, w_in_hg, w_mem_kv, lower_bounds, hg_norm_g, w_out,
           ln_mix_g, ln_mix_b, w_up, w_down, ln_ffn_g, ln_ffn_b):
    b, s, d = x.shape
    t = b * s
    tm = 512
    xf = x.reshape(t, d)
    mem2d = mem.reshape(b * mem.shape[1], d)
    qk_scale = SB_HD ** -0.5 * LOG2E
    mem_scale = MEM_HD ** -0.5
    for layer in range(DEPTH):
        slot = layer // N_MIXERS
        if layer % N_MIXERS == 0:
            q, k, v, qm = _proj(xf, w_in_sb[slot].astype(BF16),
                                (W_SB, W_SB, W_SB, W_MEM), (qk_scale, 1.0, 1.0, mem_scale), tm)
            mix = _sb_attention(q.reshape(b, s, W_SB), k.reshape(b, s, W_SB),
                                v.reshape(b, s, W_SB))
        else:
            q, f, i, gate, qm = _proj(xf, w_in_hg[slot].astype(BF16),
                                      (W_HG, W_HG, W_HG, W_HG, W_MEM),
                                      (1.0, 1.0, 1.0, 1.0, mem_scale), tm)
            mix = _hgrn_mixer(q.reshape(b, s, W_HG), f.reshape(b, s, W_HG),
                              i.reshape(b, s, W_HG), gate.reshape(b, s, W_HG),
                              lower_bounds, hg_norm_g[slot].reshape(1, W_HG), layer)
        (kv,) = _proj(mem2d, w_mem_kv[layer].astype(BF16), (2 * W_MEM,), (1.0,), N_MEM)
        x1 = _post(xf, mix.reshape(t, W_SB), qm, kv.reshape(b, N_MEM, 2 * W_MEM),
                   w_out[layer].astype(BF16), ln_mix_g[layer].reshape(1, d),
                   ln_mix_b[layer].reshape(1, d), s, tm)
        xf = _mlp(x1, w_up[layer].astype(BF16), w_down[layer].astype(BF16),
                  ln_ffn_g[layer].reshape(1, d), ln_ffn_b[layer].reshape(1, d), tm)
    return xf.reshape(b, s, d)
```
